```python
import jax, jax.numpy as jnp
from jax import lax
import numpy as np

D_MODEL = 2048
BATCH = 4
SEQ = 4096
DEPTH = 1

N_BRANCH = 2
BRANCH_WIDTH = 2048
GMLP_WIDTH = BRANCH_WIDTH
GMLP_GROUPS = 8
GMLP_GROUP_DIM = GMLP_WIDTH // GMLP_GROUPS
GMLP_CHUNK = 128
SSD_WIDTH = BRANCH_WIDTH
SSD_HEAD_DIM = 64
SSD_HEADS = SSD_WIDTH // SSD_HEAD_DIM
SSD_GROUPS = 8
SSD_HEADS_PER_GROUP = SSD_HEADS // SSD_GROUPS
SSD_STATE = 128
SSD_CONV = 4
SSD_CHUNK = 128
SSD_CONV_DIM = SSD_WIDTH + 2 * SSD_GROUPS * SSD_STATE
SSD_NORM_GROUP = SSD_WIDTH // SSD_GROUPS
IN_PROJ_DIM = 3 * GMLP_WIDTH + SSD_WIDTH + SSD_CONV_DIM + SSD_HEADS + N_BRANCH * D_MODEL
EPS = 1e-5

kernel_name = "hybrid_gmlp_ssd_gated_merge"


def rms_norm(x, w):
    xf = x.astype(jnp.float32)
    y = xf * lax.rsqrt(jnp.mean(xf * xf, axis=-1, keepdims=True) + EPS)
    return (y * w.astype(jnp.float32)).astype(x.dtype)


def layer_norm(x, w, b):
    xf = x.astype(jnp.float32)
    mu = jnp.mean(xf, axis=-1, keepdims=True)
    var = jnp.mean(jnp.square(xf - mu), axis=-1, keepdims=True)
    y = (xf - mu) * lax.rsqrt(var + EPS)
    return (y * w.astype(jnp.float32) + b.astype(jnp.float32)).astype(x.dtype)


def gmlp_spatial_gating(u, v, z, ln_w, ln_b, w_s, b_s):
    b, s, _ = v.shape
    nc = s // GMLP_CHUNK
    v = layer_norm(v, ln_w, ln_b).reshape(b, nc, GMLP_CHUNK, GMLP_GROUPS, GMLP_GROUP_DIM)
    causal = jnp.tril(jnp.ones((GMLP_CHUNK, GMLP_CHUNK), dtype=bool))
    w_masked = jnp.where(causal[None], w_s, jnp.zeros_like(w_s))
    mixed = jnp.einsum('gts,bcsgd->bctgd', w_masked, v) + b_s.T[None, None, :, :, None]
    return jax.nn.silu(z) * u * mixed.reshape(b, s, GMLP_WIDTH)


def causal_depthwise_conv(u, w, bias):
    k_w = w.shape[0]
    s = u.shape[1]
    up = jnp.pad(u, ((0, 0), (k_w - 1, 0), (0, 0)))
    out = bias
    for k in range(k_w):
        out = out + up[:, k:k + s] * w[k]
    return out


def ssd_chunked_scan(xs, dt, a, bm, cm):
    b, s = xs.shape[:2]
    nc = s // SSD_CHUNK
    L, G, K, P, N = SSD_CHUNK, SSD_GROUPS, SSD_HEADS_PER_GROUP, SSD_HEAD_DIM, SSD_STATE
    x = (xs * dt[..., None]).reshape(b, nc, L, G, K, P)
    adt = (dt * a).reshape(b, nc, L, G, K).astype(jnp.float32)
    bm = bm.reshape(b, nc, L, G, N)
    cm = cm.reshape(b, nc, L, G, N)
    a_cum = jnp.cumsum(adt, axis=2)
    a_cum_t = jnp.moveaxis(a_cum, 2, -1)
    causal = jnp.tril(jnp.ones((L, L), dtype=bool))
    seg = a_cum_t[..., :, None] - a_cum_t[..., None, :]
    decay = jnp.where(causal, jnp.exp(jnp.where(causal, seg, jnp.zeros_like(seg))), jnp.zeros_like(seg))
    cb = jnp.einsum('bclgn,bcsgn->bcgls', cm, bm)
    scores = cb[:, :, :, None] * decay.astype(cb.dtype)
    y_diag = jnp.einsum('bcgkls,bcsgkp->bclgkp', scores, x)
    decay_to_end = jnp.exp(a_cum[:, :, -1:] - a_cum).astype(x.dtype)
    states = jnp.einsum('bclgn,bclgkp->bcgkpn', bm, x * decay_to_end[..., None])
    chunk_decay = jnp.exp(a_cum_t[..., -1]).astype(states.dtype)

    def step(h, inp):
        st, dec = inp
        return h * dec[..., None, None] + st, h

    h0 = jnp.zeros((b, G, K, P, N), dtype=states.dtype)
    _, prev = lax.scan(step, h0, (jnp.moveaxis(states, 1, 0), jnp.moveaxis(chunk_decay, 1, 0)))
    prev = jnp.moveaxis(prev, 0, 1)
    y_off = jnp.einsum('bclgn,bcgkpn->bclgkp', cm, prev) * jnp.exp(a_cum).astype(x.dtype)[..., None]
    return (y_diag + y_off).reshape(b, s, SSD_HEADS, P)


def ssd_branch(z, xbc, dt_raw, conv_w, conv_b, dt_bias, a_log, d_skip, norm_w):
    b, s, _ = xbc.shape
    xbc = jax.nn.silu(causal_depthwise_conv(xbc, conv_w, conv_b))
    gn = SSD_GROUPS * SSD_STATE
    xs = xbc[..., :SSD_WIDTH].reshape(b, s, SSD_HEADS, SSD_HEAD_DIM)
    bm = xbc[..., SSD_WIDTH:SSD_WIDTH + gn].reshape(b, s, SSD_GROUPS, SSD_STATE)
    cm = xbc[..., SSD_WIDTH + gn:].reshape(b, s, SSD_GROUPS, SSD_STATE)
    dt = jax.nn.softplus(dt_raw + dt_bias)
    a = -jnp.exp(a_log)
    y = ssd_chunked_scan(xs, dt, a, bm, cm) + d_skip[:, None] * xs
    y = y.reshape(b, s, SSD_WIDTH) * jax.nn.silu(z)
    yf = y.astype(jnp.float32).reshape(b, s, SSD_GROUPS, SSD_NORM_GROUP)
    yf = yf * lax.rsqrt(jnp.mean(yf * yf, axis=-1, keepdims=True) + EPS)
    return (yf.reshape(b, s, SSD_WIDTH) * norm_w.astype(jnp.float32)).astype(y.dtype)


def setup_inputs(seed: int = 0) -> dict:
    key = jax.random.key(seed)
    ks = jax.random.split(key, 17)
    nrm = jax.random.normal
    x = nrm(ks[0], (BATCH, SEQ, D_MODEL), jnp.float32)
    norm_w = 1.0 + 0.02 * nrm(ks[1], (DEPTH, D_MODEL), jnp.float32)
    w_in = nrm(ks[2], (DEPTH, D_MODEL, IN_PROJ_DIM), jnp.float32) * D_MODEL ** -0.5
    b_gate = 0.02 * nrm(ks[3], (DEPTH, N_BRANCH, D_MODEL), jnp.float32)
    ln_v_w = 1.0 + 0.02 * nrm(ks[4], (DEPTH, GMLP_WIDTH), jnp.float32)
    ln_v_b = 0.02 * nrm(ks[5], (DEPTH, GMLP_WIDTH), jnp.float32)
    w_spatial = nrm(ks[6], (DEPTH, GMLP_GROUPS, GMLP_CHUNK, GMLP_CHUNK), jnp.float32) * GMLP_CHUNK ** -0.5
    b_spatial = 1.0 + 0.1 * nrm(ks[7], (DEPTH, GMLP_GROUPS, GMLP_CHUNK), jnp.float32)
    conv_w = jax.random.uniform(ks[8], (DEPTH, SSD_CONV, SSD_CONV_DIM), jnp.float32, -1.0, 1.0) * SSD_CONV ** -0.5
    conv_b = 0.02 * nrm(ks[9], (DEPTH, SSD_CONV_DIM), jnp.float32)
    dt0 = jnp.exp(jax.random.uniform(ks[10], (DEPTH, SSD_HEADS), jnp.float32, np.log(1e-3), np.log(1e-1)))
    dt_bias = dt0 + jnp.log(-jnp.expm1(-dt0))
    a_log = jnp.log(jax.random.uniform(ks[11], (DEPTH, SSD_HEADS), jnp.float32, 1.0, 16.0))
    d_skip = 1.0 + 0.1 * nrm(ks[12], (DEPTH, SSD_HEADS), jnp.float32)
    ssm_norm_w = 1.0 + 0.02 * nrm(ks[13], (DEPTH, SSD_WIDTH), jnp.float32)
    w_branch = nrm(ks[14], (DEPTH, N_BRANCH, BRANCH_WIDTH, D_MODEL), jnp.float32) * BRANCH_WIDTH ** -0.5
    w_out = nrm(ks[15], (DEPTH, D_MODEL, D_MODEL), jnp.float32) * D_MODEL ** -0.5
    final_norm_w = 1.0 + 0.02 * nrm(ks[16], (D_MODEL,), jnp.float32)
    return {"x": x, "norm_w": norm_w, "w_in": w_in, "b_gate": b_gate, "ln_v_w": ln_v_w, "ln_v_b": ln_v_b,
            "w_spatial": w_spatial, "b_spatial": b_spatial, "conv_w": conv_w, "conv_b": conv_b,
            "dt_bias": dt_bias, "a_log": a_log, "d_skip": d_skip, "ssm_norm_w": ssm_norm_w,
            "w_branch": w_branch, "w_out": w_out, "final_norm_w": final_norm_w}


def reference(x, norm_w, w_in, b_gate, ln_v_w, ln_v_b, w_spatial, b_spatial, conv_w, conv_b,
              dt_bias, a_log, d_skip, ssm_norm_w, w_branch, w_out, final_norm_w):
    b, s, _ = x.shape
    o1 = GMLP_WIDTH
    o2 = 2 * GMLP_WIDTH
    o3 = 3 * GMLP_WIDTH
    o4 = o3 + SSD_WIDTH
    o5 = o4 + SSD_CONV_DIM
    o6 = o5 + SSD_HEADS
    for l in range(DEPTH):
        h = rms_norm(x, norm_w[l])
        proj = h @ w_in[l]
        u, v, z_a = proj[..., :o1], proj[..., o1:o2], proj[..., o2:o3]
        z_b, xbc, dt_raw = proj[..., o3:o4], proj[..., o4:o5], proj[..., o5:o6]
        gate_logits = proj[..., o6:].reshape(b, s, N_BRANCH, D_MODEL)
        y_a = gmlp_spatial_gating(u, v, z_a, ln_v_w[l], ln_v_b[l], w_spatial[l], b_spatial[l])
        y_b = ssd_branch(z_b, xbc, dt_raw, conv_w[l], conv_b[l], dt_bias[l], a_log[l], d_skip[l], ssm_norm_w[l])
        branches = jnp.stack([y_a, y_b], axis=2)
        branch_d = jnp.einsum('bsne,ned->bsnd', branches, w_branch[l])
        gates = jax.nn.sigmoid(gate_logits + b_gate[l])
        merged = jnp.sum(gates * branch_d, axis=2)
        x = x + merged @ w_out[l]
    return rms_norm(x, final_norm_w)
```

```python
import functools

import jax
import jax.numpy as jnp
from jax import lax
from jax.experimental import pallas as pl
from jax.experimental.pallas import tpu as pltpu

F32 = jnp.float32
BF16 = jnp.bfloat16

EPS = 1e-5
GMLP_GROUPS = 8
GMLP_CHUNK = 128
SSD_HEAD_DIM = 64
SSD_GROUPS = 8
SSD_STATE = 128
SSD_CHUNK = 128
N_BRANCH = 2

LANES = 128
SUBLANES = 8
SPLIT_TERMS = 3

INPROJ_TM = 1024
INPROJ_TN = 1024
GMLP_TB = 512
MERGE_TM = 256
VMEM_LIMIT = 60000 * 1024


def _silu(x):
    return x * (1.0 / (1.0 + jnp.exp(-x)))


def _sigmoid(x):
    return 1.0 / (1.0 + jnp.exp(-x))


def _softplus(x):
    return jnp.maximum(x, 0.0) + jnp.log1p(jnp.exp(-jnp.abs(x)))


def _split_bf16(v, n_terms):
    terms = []
    r = v
    for _ in range(n_terms):
        t = r.astype(BF16).astype(F32)
        terms.append(t)
        r = r - t
    return terms


def _inproj_kernel(x_ref, nw_ref, w_ref, wdt_ref, proj_ref, dt_ref, h_ref, *, row_chunk):
    j = pl.program_id(1)
    tm = x_ref.shape[0]

    @pl.when(j == 0)
    def _norm():
        def body(r, carry):
            r0 = pl.multiple_of(r * row_chunk, row_chunk)
            x = x_ref[pl.ds(r0, row_chunk), :]
            ms = jnp.mean(x * x, axis=-1, keepdims=True)
            h = x * lax.rsqrt(ms + EPS) * nw_ref[...]
            h_ref[pl.ds(r0, row_chunk), :] = h.astype(BF16)
            return carry
        lax.fori_loop(0, tm // row_chunk, body, 0)
        dt_ref[...] = jnp.dot(h_ref[...], wdt_ref[...], preferred_element_type=F32)

    proj_ref[...] = jnp.dot(h_ref[...], w_ref[...],
                            preferred_element_type=F32).astype(proj_ref.dtype)


def _inproj(x2d, norm_w, w_main, w_dt):
    t, d = x2d.shape
    n = w_main.shape[1]
    tm, tn = INPROJ_TM, INPROJ_TN
    assert t % tm == 0 and n % tn == 0
    return pl.pallas_call(
        functools.partial(_inproj_kernel, row_chunk=128),
        grid=(t // tm, n // tn),
        in_specs=[
            pl.BlockSpec((tm, d), lambda i, j: (i, 0)),
            pl.BlockSpec((1, d), lambda i, j: (0, 0)),
            pl.BlockSpec((d, tn), lambda i, j: (0, j)),
            pl.BlockSpec((d, LANES), lambda i, j: (0, 0)),
        ],
        out_specs=[
            pl.BlockSpec((tm, tn), lambda i, j: (i, j)),
            pl.BlockSpec((tm, LANES), lambda i, j: (i, 0)),
        ],
        out_shape=[
            jax.ShapeDtypeStruct((t, n), BF16),
            jax.ShapeDtypeStruct((t, LANES), F32),
        ],
        scratch_shapes=[pltpu.VMEM((tm, d), BF16)],
        compiler_params=pltpu.CompilerParams(
            dimension_semantics=("arbitrary", "arbitrary"),
            vmem_limit_bytes=VMEM_LIMIT),
        name="inproj",
    )(x2d, norm_w, w_main, w_dt)


def _gmlp_kernel(u_ref, v_ref, z_ref, lnw_ref, lnb_ref, ws_ref, bst_ref, y_ref):
    tb, e = u_ref.shape
    l = GMLP_CHUNK
    gd = e // GMLP_GROUPS
    row = lax.broadcasted_iota(jnp.int32, (l, l), 0)
    col = lax.broadcasted_iota(jnp.int32, (l, l), 1)
    causal = col <= row
    wm = [jnp.where(causal, ws_ref[g], 0.0).astype(BF16) for g in range(GMLP_GROUPS)]
    lnw = lnw_ref[...]
    lnb = lnb_ref[...]

    def chunk(ci, carry):
        r0 = pl.multiple_of(ci * l, l)
        v = v_ref[pl.ds(r0, l), :].astype(F32)
        mu = jnp.mean(v, axis=-1, keepdims=True)
        dv = v - mu
        var = jnp.mean(dv * dv, axis=-1, keepdims=True)
        vn = (dv * lax.rsqrt(var + EPS) * lnw + lnb).astype(BF16)
        for g in range(GMLP_GROUPS):
            cs = slice(g * gd, (g + 1) * gd)
            mixed = jnp.dot(wm[g], vn[:, cs], preferred_element_type=F32) + bst_ref[:, g:g + 1]
            z = z_ref[pl.ds(r0, l), cs].astype(F32)
            u = u_ref[pl.ds(r0, l), cs].astype(F32)
            y_ref[pl.ds(r0, l), cs] = (_silu(z) * u * mixed).astype(y_ref.dtype)
        return carry

    lax.fori_loop(0, tb // l, chunk, 0)


def _gmlp(proj, ln_w, ln_b, w_s, b_s_t, width):
    t = proj.shape[0]
    tb = GMLP_TB
    g, l, _ = w_s.shape
    assert l == GMLP_CHUNK and g == GMLP_GROUPS and t % tb == 0
    return pl.pallas_call(
        _gmlp_kernel,
        grid=(t // tb,),
        in_specs=[
            pl.BlockSpec((tb, width), lambda i: (i, 0)),
            pl.BlockSpec((tb, width), lambda i: (i, 1)),
            pl.BlockSpec((tb, width), lambda i: (i, 2)),
            pl.BlockSpec((1, width), lambda i: (0, 0)),
            pl.BlockSpec((1, width), lambda i: (0, 0)),
            pl.BlockSpec((g, l, l), lambda i: (0, 0, 0)),
            pl.BlockSpec((l, g), lambda i: (0, 0)),
        ],
        out_specs=pl.BlockSpec((tb, width), lambda i: (i, 0)),
        out_shape=jax.ShapeDtypeStruct((t, width), BF16),
        compiler_params=pltpu.CompilerParams(
            dimension_semantics=("arbitrary",), vmem_limit_bytes=VMEM_LIMIT),
        name="gmlp",
    )(proj, proj, proj, ln_w, ln_b, w_s, b_s_t)


def _ssd_kernel(zb_ref, xbc_ref, dtr_ref, cw_ref, cbias_ref, dtb_ref, alog_ref, dskip_ref,
                nw_ref, e3_ref, y_ref, xpad_ref, xc_ref, ht_ref, we_ref, cde_ref,
                *, n_heads, conv_k):
    c = pl.program_id(1)
    l = SSD_CHUNK
    width = zb_ref.shape[1]
    conv_dim = xbc_ref.shape[1]
    n = SSD_STATE
    g_n = SSD_GROUPS
    gw = width // g_n
    hpg = gw // SSD_HEAD_DIM
    pad = SUBLANES

    @pl.when(c == 0)
    def _init():
        xpad_ref[0:pad, :] = jnp.zeros((pad, conv_dim), F32)
        ht_ref[...] = jnp.zeros(ht_ref.shape, F32)

    xpad_ref[pad:pad + l, :] = xbc_ref[...].astype(F32)
    acc = cbias_ref[...]
    for k in range(conv_k):
        off = pad - (conv_k - 1) + k
        acc = acc + xpad_ref[off:off + l, :] * cw_ref[k:k + 1, :]
    xc_ref[...] = _silu(acc)
    xpad_ref[0:pad, :] = xpad_ref[l:l + pad, :]

    lane = lax.broadcasted_iota(jnp.int32, (l, LANES), 1)
    row = lax.broadcasted_iota(jnp.int32, (l, LANES), 0)
    causal = lane <= row
    tril = causal.astype(BF16)
    dt = _softplus(dtr_ref[...] + dtb_ref[...])
    a = -jnp.exp(alog_ref[...])
    adt = dt * a
    a_col = None
    for term in _split_bf16(adt, SPLIT_TERMS):
        part = jnp.dot(tril, term.astype(BF16), preferred_element_type=F32)
        a_col = part if a_col is None else a_col + part
    a_row = a_col.T
    dt_row = dt.T
    a_last = a_col[l - 1:l, :]
    head_valid = lane < n_heads
    w2 = jnp.where(head_valid, dt * jnp.exp(a_last - a_col), 0.0)
    w3 = jnp.where(head_valid, jnp.exp(a_col), 0.0)
    cdec = jnp.where(head_valid[0:pad], jnp.broadcast_to(jnp.exp(a_last), (pad, LANES)), 0.0)

    def pack(v):
        hi, mid, lo = _split_bf16(v, SPLIT_TERMS)
        return (hi + pltpu.roll(mid, n_heads, axis=1)
                + pltpu.roll(lo, 2 * n_heads, axis=1)).astype(BF16)

    e3 = e3_ref[...]
    we_ref[0:l, :] = jnp.dot(pack(w2), e3, preferred_element_type=F32)
    we_ref[l:2 * l, :] = jnp.dot(pack(w3), e3, preferred_element_type=F32)
    cde_ref[...] = jnp.dot(pack(cdec), e3, preferred_element_type=F32)

    lane_g = lax.broadcasted_iota(jnp.int32, (l, gw), 1)
    for g in range(g_n):
        gs = slice(g * gw, (g + 1) * gw)
        b_g = xc_ref[:, width + g * n: width + (g + 1) * n].astype(BF16)
        c_g = xc_ref[:, width + g_n * n + g * n: width + g_n * n + (g + 1) * n].astype(BF16)
        cb = lax.dot_general(c_g, b_g, (((1,), (1,)), ((), ())), preferred_element_type=F32)
        xs_g = xc_ref[:, gs]
        xs_bf = xs_g.astype(BF16)

        scores = []
        xbd = []
        for k in range(hpg):
            h = g * hpg + k
            seg = a_col[:, h:h + 1] - a_row[h:h + 1, :]
            dec = jnp.where(causal, jnp.exp(jnp.where(causal, seg, 0.0)), 0.0)
            scores.append((cb * dec * dt_row[h:h + 1, :]).astype(BF16))
            in_head = (lane_g >= k * SSD_HEAD_DIM) & (lane_g < (k + 1) * SSD_HEAD_DIM)
            xbd.append(jnp.where(in_head, xs_bf, jnp.zeros_like(xs_bf)))
        s_cat = jnp.concatenate(scores, axis=1)
        x_bd = jnp.concatenate(xbd, axis=0)
        y_diag = jnp.dot(s_cat, x_bd, preferred_element_type=F32)

        h_prev = ht_ref[g]
        y_off = jnp.dot(c_g, h_prev.astype(BF16), preferred_element_type=F32) * we_ref[l:2 * l, gs]
        x_dec = (xs_g * we_ref[0:l, gs]).astype(BF16)
        st = lax.dot_general(b_g, x_dec, (((0,), (0,)), ((), ())), preferred_element_type=F32)
        ht_ref[g] = h_prev * cde_ref[0:1, gs] + st

        y = y_diag + y_off + dskip_ref[:, gs] * xs_g
        y = y * _silu(zb_ref[:, gs].astype(F32))
        ms = jnp.mean(y * y, axis=-1, keepdims=True)
        y_ref[:, gs] = (y * lax.rsqrt(ms + EPS) * nw_ref[:, gs]).astype(y_ref.dtype)


def _ssd(proj, dt_raw, conv_w, conv_b, dtb_pad, alog_pad, dskip_e, norm_w, e3, batch, seq, width,
         n_heads):
    t = proj.shape[0]
    l = SSD_CHUNK
    nc = seq // l
    conv_k, conv_dim = conv_w.shape
    assert conv_dim == width + 2 * SSD_GROUPS * SSD_STATE and conv_dim % width == 0
    assert width % SSD_GROUPS == 0 and SPLIT_TERMS * n_heads <= LANES and conv_k - 1 <= SUBLANES
    zb_blk = 3
    xbc_blk = (4 * width) // conv_dim
    tok = lambda b, c: b * nc + c
    const = lambda b, c: (0, 0)
    return pl.pallas_call(
        functools.partial(_ssd_kernel, n_heads=n_heads, conv_k=conv_k),
        grid=(batch, nc),
        in_specs=[
            pl.BlockSpec((l, width), lambda b, c: (tok(b, c), zb_blk)),
            pl.BlockSpec((l, conv_dim), lambda b, c: (tok(b, c), xbc_blk)),
            pl.BlockSpec((l, LANES), lambda b, c: (tok(b, c), 0)),
            pl.BlockSpec((conv_k, conv_dim), const),
            pl.BlockSpec((1, conv_dim), const),
            pl.BlockSpec((1, LANES), const),
            pl.BlockSpec((1, LANES), const),
            pl.BlockSpec((1, width), const),
            pl.BlockSpec((1, width), const),
            pl.BlockSpec((LANES, width), const),
        ],
        out_specs=pl.BlockSpec((l, width), lambda b, c: (tok(b, c), 0)),
        out_shape=jax.ShapeDtypeStruct((t, width), BF16),
        scratch_shapes=[
            pltpu.VMEM((SUBLANES + l + SUBLANES, conv_dim), F32),
            pltpu.VMEM((l, conv_dim), F32),
            pltpu.VMEM((SSD_GROUPS, SSD_STATE, width // SSD_GROUPS), F32),
            pltpu.VMEM((2 * l, width), F32),
            pltpu.VMEM((SUBLANES, width), F32),
        ],
        compiler_params=pltpu.CompilerParams(
            dimension_semantics=("arbitrary", "arbitrary"), vmem_limit_bytes=VMEM_LIMIT),
        name="ssd",
    )(proj, proj, dt_raw, conv_w, conv_b, dtb_pad, alog_pad, dskip_e, norm_w, e3)


def _merge_kernel(ya_ref, yb_ref, gl_ref, x_ref, bg_ref, wb_ref, wo_ref, fnw_ref, o_ref, *, final):
    d = x_ref.shape[1]
    bd0 = jnp.dot(ya_ref[...], wb_ref[0], preferred_element_type=F32)
    bd1 = jnp.dot(yb_ref[...], wb_ref[1], preferred_element_type=F32)
    g0 = _sigmoid(gl_ref[:, 0:d].astype(F32) + bg_ref[0:1, :])
    g1 = _sigmoid(gl_ref[:, d:2 * d].astype(F32) + bg_ref[1:2, :])
    merged = (g0 * bd0 + g1 * bd1).astype(BF16)
    xn = x_ref[...] + jnp.dot(merged, wo_ref[...], preferred_element_type=F32)
    if final:
        ms = jnp.mean(xn * xn, axis=-1, keepdims=True)
        xn = xn * lax.rsqrt(ms + EPS) * fnw_ref[...]
    o_ref[...] = xn


def _merge(y_a, y_b, proj, x2d, b_gate, w_branch, w_out, final_norm_w, final):
    t, d = x2d.shape
    e = y_a.shape[1]
    tm = MERGE_TM
    gate_blk = (proj.shape[1] - N_BRANCH * d) // (N_BRANCH * d)
    assert t % tm == 0 and gate_blk * N_BRANCH * d == proj.shape[1] - N_BRANCH * d
    resident = dict(pipeline_mode=pl.Buffered(1))
    return pl.pallas_call(
        functools.partial(_merge_kernel, final=final),
        grid=(t // tm,),
        in_specs=[
            pl.BlockSpec((tm, e), lambda i: (i, 0)),
            pl.BlockSpec((tm, e), lambda i: (i, 0)),
            pl.BlockSpec((tm, N_BRANCH * d), lambda i: (i, gate_blk)),
            pl.BlockSpec((tm, d), lambda i: (i, 0)),
            pl.BlockSpec((N_BRANCH, d), lambda i: (0, 0)),
            pl.BlockSpec((N_BRANCH, e, d), lambda i: (0, 0, 0), **resident),
            pl.BlockSpec((d, d), lambda i: (0, 0), **resident),
            pl.BlockSpec((1, d), lambda i: (0, 0)),
        ],
        out_specs=pl.BlockSpec((tm, d), lambda i: (i, 0)),
        out_shape=jax.ShapeDtypeStruct((t, d), F32),
        compiler_params=pltpu.CompilerParams(
            dimension_semantics=("arbitrary",), vmem_limit_bytes=VMEM_LIMIT),
        name="merge",
    )(y_a, y_b, proj, x2d, b_gate, w_branch, w_out, final_norm_w)


def kernel(x, norm_w, w_in, b_gate, ln_v_w, ln_v_b, w_spatial, b_spatial, conv_w, conv_b,
           dt_bias, a_log, d_skip, ssm_norm_w, w_branch, w_out, final_norm_w):
    batch, seq, d = x.shape
    depth = norm_w.shape[0]
    width = ln_v_w.shape[1]
    n_heads = dt_bias.shape[1]
    conv_dim = conv_w.shape[2]
    assert width == d and n_heads * SSD_HEAD_DIM == width
    o_dt = 4 * width + conv_dim
    t = batch * seq
    x2d = x.reshape(t, d)

    head_of_col = jnp.arange(width, dtype=jnp.int32) // SSD_HEAD_DIM
    e_rows = jnp.arange(LANES, dtype=jnp.int32)
    e3 = ((e_rows[:, None] % n_heads == head_of_col[None, :])
          & (e_rows[:, None] < SPLIT_TERMS * n_heads)).astype(BF16)

    for layer in range(depth):
        w_l = w_in[layer]
        w_main = jnp.concatenate([w_l[:, :o_dt], w_l[:, o_dt + n_heads:]], axis=1).astype(BF16)
        w_dt = jnp.pad(w_l[:, o_dt:o_dt + n_heads], ((0, 0), (0, LANES - n_heads))).astype(BF16)
        proj, dt_raw = _inproj(x2d, norm_w[layer][None, :], w_main, w_dt)

        y_a = _gmlp(proj, ln_v_w[layer][None, :], ln_v_b[layer][None, :],
                    w_spatial[layer], b_spatial[layer].T, width)

        pad_h = ((0, 0), (0, LANES - n_heads))
        y_b = _ssd(proj, dt_raw, conv_w[layer], conv_b[layer][None, :],
                   jnp.pad(dt_bias[layer][None, :], pad_h), jnp.pad(a_log[layer][None, :], pad_h),
                   jnp.repeat(d_skip[layer], SSD_HEAD_DIM)[None, :], ssm_norm_w[layer][None, :],
                   e3, batch, seq, width, n_heads)

        x2d = _merge(y_a, y_b, proj, x2d, b_gate[layer], w_branch[layer].astype(BF16),
                     w_out[layer].astype(BF16), final_norm_w[None, :], final=(layer == depth - 1))
    return x2d.reshape(batch, seq, d)
```

```python
import functools

import jax
import jax.numpy as jnp
from jax import lax
from jax.experimental import pallas as pl
from jax.experimental.pallas import tpu as pltpu

F32 = jnp.float32
BF16 = jnp.bfloat16

EPS = 1e-5
GMLP_GROUPS = 8
GMLP_CHUNK = 128
SSD_HEAD_DIM = 64
SSD_GROUPS = 8
SSD_STATE = 128
SSD_CHUNK = 128
N_BRANCH = 2

LANES = 128
SUBLANES = 8
SPLIT_TERMS = 3

INPROJ_TM = 1024
INPROJ_TN = 2048
GMLP_TB = 512
MERGE_TM = 256
VMEM_LIMIT = 60000 * 1024


def _silu(x):
    hx = 0.5 * x
    return hx + hx * jnp.tanh(hx)


def _sigmoid(x):
    return 1.0 / (1.0 + jnp.exp(-x))


def _softplus(x):
    return jnp.maximum(x, 0.0) + jnp.log1p(jnp.exp(-jnp.abs(x)))


def _split_bf16(v, n_terms):
    terms = []
    r = v
    for _ in range(n_terms):
        t = r.astype(BF16).astype(F32)
        terms.append(t)
        r = r - t
    return terms


def _inproj_kernel(x_ref, nw_ref, w_ref, wdt_ref, proj_ref, dt_ref, h_ref, *, row_chunk):
    j = pl.program_id(1)
    tm = x_ref.shape[0]

    @pl.when(j == 0)
    def _norm():
        def body(r, carry):
            r0 = pl.multiple_of(r * row_chunk, row_chunk)
            x = x_ref[pl.ds(r0, row_chunk), :]
            ms = jnp.mean(x * x, axis=-1, keepdims=True)
            h = x * lax.rsqrt(ms + EPS) * nw_ref[...]
            h_ref[pl.ds(r0, row_chunk), :] = h.astype(BF16)
            return carry
        lax.fori_loop(0, tm // row_chunk, body, 0)
        dt_ref[...] = jnp.dot(h_ref[...], wdt_ref[...], preferred_element_type=F32)

    proj_ref[...] = jnp.dot(h_ref[...], w_ref[...],
                            preferred_element_type=F32).astype(proj_ref.dtype)


def _inproj(x2d, norm_w, w_all, w_dt, n):
    t, d = x2d.shape
    tm, tn = INPROJ_TM, INPROJ_TN
    assert t % tm == 0 and n % tn == 0 and n <= w_all.shape[1]
    return pl.pallas_call(
        functools.partial(_inproj_kernel, row_chunk=128),
        grid=(t // tm, n // tn),
        in_specs=[
            pl.BlockSpec((tm, d), lambda i, j: (i, 0)),
            pl.BlockSpec((1, d), lambda i, j: (0, 0)),
            pl.BlockSpec((d, tn), lambda i, j: (0, j)),
            pl.BlockSpec((d, LANES), lambda i, j: (0, 0)),
        ],
        out_specs=[
            pl.BlockSpec((tm, tn), lambda i, j: (i, j)),
            pl.BlockSpec((tm, LANES), lambda i, j: (i, 0)),
        ],
        out_shape=[
            jax.ShapeDtypeStruct((t, n), BF16),
            jax.ShapeDtypeStruct((t, LANES), F32),
        ],
        scratch_shapes=[pltpu.VMEM((tm, d), BF16)],
        compiler_params=pltpu.CompilerParams(
            dimension_semantics=("arbitrary", "arbitrary"),
            vmem_limit_bytes=VMEM_LIMIT),
        name="inproj",
    )(x2d, norm_w, w_all, w_dt)


def _gmlp_kernel(u_ref, v_ref, z_ref, lnw_ref, lnb_ref, ws_ref, bst_ref, y_ref):
    tb, e = u_ref.shape
    l = GMLP_CHUNK
    gd = e // GMLP_GROUPS
    row = lax.broadcasted_iota(jnp.int32, (l, l), 0)
    col = lax.broadcasted_iota(jnp.int32, (l, l), 1)
    causal = col <= row
    wm = [jnp.where(causal, ws_ref[g], 0.0).astype(BF16) for g in range(GMLP_GROUPS)]
    lnw = lnw_ref[...]
    lnb = lnb_ref[...]

    def chunk(ci, carry):
        r0 = pl.multiple_of(ci * l, l)
        v = v_ref[pl.ds(r0, l), :].astype(F32)
        mu = jnp.mean(v, axis=-1, keepdims=True)
        dv = v - mu
        var = jnp.mean(dv * dv, axis=-1, keepdims=True)
        vn = (dv * lax.rsqrt(var + EPS) * lnw + lnb).astype(BF16)
        for g in range(GMLP_GROUPS):
            cs = slice(g * gd, (g + 1) * gd)
            mixed = jnp.dot(wm[g], vn[:, cs], preferred_element_type=F32) + bst_ref[:, g:g + 1]
            z = z_ref[pl.ds(r0, l), cs].astype(F32)
            u = u_ref[pl.ds(r0, l), cs].astype(F32)
            y_ref[pl.ds(r0, l), cs] = (_silu(z) * u * mixed).astype(y_ref.dtype)
        return carry

    lax.fori_loop(0, tb // l, chunk, 0)


def _gmlp(proj, ln_w, ln_b, w_s, b_s_t, width):
    t = proj.shape[0]
    tb = GMLP_TB
    g, l, _ = w_s.shape
    assert l == GMLP_CHUNK and g == GMLP_GROUPS and t % tb == 0
    return pl.pallas_call(
        _gmlp_kernel,
        grid=(t // tb,),
        in_specs=[
            pl.BlockSpec((tb, width), lambda i: (i, 0)),
            pl.BlockSpec((tb, width), lambda i: (i, 1)),
            pl.BlockSpec((tb, width), lambda i: (i, 2)),
            pl.BlockSpec((1, width), lambda i: (0, 0)),
            pl.BlockSpec((1, width), lambda i: (0, 0)),
            pl.BlockSpec((g, l, l), lambda i: (0, 0, 0)),
            pl.BlockSpec((l, g), lambda i: (0, 0)),
        ],
        out_specs=pl.BlockSpec((tb, width), lambda i: (i, 0)),
        out_shape=jax.ShapeDtypeStruct((t, width), BF16),
        compiler_params=pltpu.CompilerParams(
            dimension_semantics=("arbitrary",), vmem_limit_bytes=VMEM_LIMIT),
        name="gmlp",
    )(proj, proj, proj, ln_w, ln_b, w_s, b_s_t)


def _ssd_kernel(zb_ref, xbc_ref, dtr_ref, shift_ref, cw_ref, cbias_ref, dtb_ref, alogc_ref,
                dskip_ref, nw_ref, e3_ref, y_ref, xx_ref, xc_ref, ht_ref, we_ref, cde_ref,
                *, n_heads):
    c = pl.program_id(1)
    l = SSD_CHUNK
    width = zb_ref.shape[1]
    conv_k = cw_ref.shape[0]
    n = SSD_STATE
    g_n = SSD_GROUPS
    gw = width // g_n
    hpg = gw // SSD_HEAD_DIM

    @pl.when(c == 0)
    def _init():
        xx_ref[0:l, :] = jnp.zeros((l, xx_ref.shape[1]), BF16)
        ht_ref[...] = jnp.zeros(ht_ref.shape, F32)

    cur = xbc_ref[...]
    xx_ref[l:2 * l, :] = cur
    shifted = jnp.dot(shift_ref[...], xx_ref[...], preferred_element_type=F32)
    acc = cbias_ref[...]
    for k in range(conv_k - 1):
        acc = acc + shifted[k * l:(k + 1) * l, :] * cw_ref[k:k + 1, :]
    acc = acc + cur.astype(F32) * cw_ref[conv_k - 1:conv_k, :]
    xc_ref[...] = _silu(acc)
    xx_ref[0:l, :] = cur

    row = lax.broadcasted_iota(jnp.int32, (l, l), 0)
    col = lax.broadcasted_iota(jnp.int32, (l, l), 1)
    causal = col <= row
    triu = (row <= col).astype(BF16)
    dt_t = _softplus((dtr_ref[...] + dtb_ref[...]).T[0:n_heads, :])
    adt_t = dt_t * (-jnp.exp(alogc_ref[0:n_heads, :]))
    a_row = None
    for term in _split_bf16(adt_t, SPLIT_TERMS):
        part = jnp.dot(term.astype(BF16), triu, preferred_element_type=F32)
        a_row = part if a_row is None else a_row + part
    a_last = a_row[:, l - 1:l]
    w2_t = dt_t * jnp.exp(a_last - a_row)
    w3_t = jnp.exp(a_row)

    zpad = jnp.zeros((LANES - n_heads, l), F32)
    to_col = lambda v: jnp.concatenate([v, zpad], axis=0).T
    a_col = to_col(a_row)
    w3 = to_col(w3_t)

    def pack(v):
        hi, mid, lo = _split_bf16(v, SPLIT_TERMS)
        return (hi + pltpu.roll(mid, n_heads, axis=1)
                + pltpu.roll(lo, 2 * n_heads, axis=1)).astype(BF16)

    e3 = e3_ref[...]
    we_ref[0:l, :] = jnp.dot(pack(to_col(w2_t)), e3, preferred_element_type=F32)
    we_ref[l:2 * l, :] = jnp.dot(pack(w3), e3, preferred_element_type=F32)
    cdec = jnp.broadcast_to(w3[l - 1:l, :], (SUBLANES, LANES))
    cde_ref[...] = jnp.dot(pack(cdec), e3, preferred_element_type=F32)

    lane_g = lax.broadcasted_iota(jnp.int32, (l, gw), 1)
    for g in range(g_n):
        gs = slice(g * gw, (g + 1) * gw)
        b_g = xc_ref[:, width + g * n: width + (g + 1) * n].astype(BF16)
        c_g = xc_ref[:, width + g_n * n + g * n: width + g_n * n + (g + 1) * n].astype(BF16)
        cb = lax.dot_general(c_g, b_g, (((1,), (1,)), ((), ())), preferred_element_type=F32)
        cb = jnp.where(causal, cb, 0.0)
        xs_g = xc_ref[:, gs]
        xs_bf = xs_g.astype(BF16)

        scores = []
        xbd = []
        for k in range(hpg):
            h = g * hpg + k
            seg = a_col[:, h:h + 1] - a_row[h:h + 1, :]
            dec = jnp.exp(jnp.where(causal, seg, 0.0))
            scores.append((cb * dec * dt_t[h:h + 1, :]).astype(BF16))
            in_head = (lane_g >= k * SSD_HEAD_DIM) & (lane_g < (k + 1) * SSD_HEAD_DIM)
            xbd.append(jnp.where(in_head, xs_bf, jnp.zeros_like(xs_bf)))
        s_cat = jnp.concatenate(scores, axis=1)
        x_bd = jnp.concatenate(xbd, axis=0)
        y_diag = jnp.dot(s_cat, x_bd, preferred_element_type=F32)

        h_prev = ht_ref[g]
        y_off = jnp.dot(c_g, h_prev.astype(BF16), preferred_element_type=F32) * we_ref[l:2 * l, gs]
        x_dec = (xs_g * we_ref[0:l, gs]).astype(BF16)
        st = lax.dot_general(b_g, x_dec, (((0,), (0,)), ((), ())), preferred_element_type=F32)
        ht_ref[g] = h_prev * cde_ref[0:1, gs] + st

        y = y_diag + y_off + dskip_ref[:, gs] * xs_g
        y = y * _silu(zb_ref[:, gs].astype(F32))
        ms = jnp.mean(y * y, axis=-1, keepdims=True)
        y_ref[:, gs] = (y * lax.rsqrt(ms + EPS) * nw_ref[:, gs]).astype(y_ref.dtype)


def _ssd(proj, dt_raw, shift, conv_w, conv_b, dtb_pad, alog_col, dskip_e, norm_w, e3, batch, seq,
         width, n_heads):
    t = proj.shape[0]
    l = SSD_CHUNK
    nc = seq // l
    conv_k, conv_dim = conv_w.shape
    assert conv_dim == width + 2 * SSD_GROUPS * SSD_STATE and (4 * width) % conv_dim == 0
    assert width % SSD_GROUPS == 0 and SPLIT_TERMS * n_heads <= LANES and conv_k - 1 <= l
    assert shift.shape == ((conv_k - 1) * l, 2 * l)
    zb_blk = 3
    xbc_blk = (4 * width) // conv_dim
    tok = lambda b, c: b * nc + c
    const = lambda b, c: (0, 0)
    return pl.pallas_call(
        functools.partial(_ssd_kernel, n_heads=n_heads),
        grid=(batch, nc),
        in_specs=[
            pl.BlockSpec((l, width), lambda b, c: (tok(b, c), zb_blk)),
            pl.BlockSpec((l, conv_dim), lambda b, c: (tok(b, c), xbc_blk)),
            pl.BlockSpec((l, LANES), lambda b, c: (tok(b, c), 0)),
            pl.BlockSpec(shift.shape, const),
            pl.BlockSpec((conv_k, conv_dim), const),
            pl.BlockSpec((1, conv_dim), const),
            pl.BlockSpec((1, LANES), const),
            pl.BlockSpec((LANES, 1), const),
            pl.BlockSpec((1, width), const),
            pl.BlockSpec((1, width), const),
            pl.BlockSpec((LANES, width), const),
        ],
        out_specs=pl.BlockSpec((l, width), lambda b, c: (tok(b, c), 0)),
        out_shape=jax.ShapeDtypeStruct((t, width), BF16),
        scratch_shapes=[
            pltpu.VMEM((2 * l, conv_dim), BF16),
            pltpu.VMEM((l, conv_dim), F32),
            pltpu.VMEM((SSD_GROUPS, SSD_STATE, width // SSD_GROUPS), F32),
            pltpu.VMEM((2 * l, width), F32),
            pltpu.VMEM((SUBLANES, width), F32),
        ],
        compiler_params=pltpu.CompilerParams(
            dimension_semantics=("arbitrary", "arbitrary"), vmem_limit_bytes=VMEM_LIMIT),
        name="ssd",
    )(proj, proj, dt_raw, shift, conv_w, conv_b, dtb_pad, alog_col, dskip_e, norm_w, e3)


def _merge_kernel(ya_ref, yb_ref, gl_ref, x_ref, bg_ref, wb_ref, wo_ref, fnw_ref, o_ref, *, final):
    d = x_ref.shape[1]
    bd0 = jnp.dot(ya_ref[...], wb_ref[0], preferred_element_type=F32)
    bd1 = jnp.dot(yb_ref[...], wb_ref[1], preferred_element_type=F32)
    g0 = _sigmoid(gl_ref[:, 0:d].astype(F32) + bg_ref[0:1, :])
    g1 = _sigmoid(gl_ref[:, d:2 * d].astype(F32) + bg_ref[1:2, :])
    merged = (g0 * bd0 + g1 * bd1).astype(BF16)
    xn = x_ref[...] + jnp.dot(merged, wo_ref[...], preferred_element_type=F32)
    if final:
        ms = jnp.mean(xn * xn, axis=-1, keepdims=True)
        xn = xn * lax.rsqrt(ms + EPS) * fnw_ref[...]
    o_ref[...] = xn


def _merge(y_a, y_b, proj, x2d, b_gate, w_branch, w_out, final_norm_w, final):
    t, d = x2d.shape
    e = y_a.shape[1]
    tm = MERGE_TM
    gate_blk = (proj.shape[1] - N_BRANCH * d) // (N_BRANCH * d)
    assert t % tm == 0 and gate_blk * N_BRANCH * d == proj.shape[1] - N_BRANCH * d
    resident = dict(pipeline_mode=pl.Buffered(1))
    return pl.pallas_call(
        functools.partial(_merge_kernel, final=final),
        grid=(t // tm,),
        in_specs=[
            pl.BlockSpec((tm, e), lambda i: (i, 0)),
            pl.BlockSpec((tm, e), lambda i: (i, 0)),
            pl.BlockSpec((tm, N_BRANCH * d), lambda i: (i, gate_blk)),
            pl.BlockSpec((tm, d), lambda i: (i, 0)),
            pl.BlockSpec((N_BRANCH, d), lambda i: (0, 0)),
            pl.BlockSpec((N_BRANCH, e, d), lambda i: (0, 0, 0), **resident),
            pl.BlockSpec((d, d), lambda i: (0, 0), **resident),
            pl.BlockSpec((1, d), lambda i: (0, 0)),
        ],
        out_specs=pl.BlockSpec((tm, d), lambda i: (i, 0)),
        out_shape=jax.ShapeDtypeStruct((t, d), F32),
        compiler_params=pltpu.CompilerParams(
            dimension_semantics=("arbitrary",), vmem_limit_bytes=VMEM_LIMIT),
        name="merge",
    )(y_a, y_b, proj, x2d, b_gate, w_branch, w_out, final_norm_w)


def kernel(x, norm_w, w_in, b_gate, ln_v_w, ln_v_b, w_spatial, b_spatial, conv_w, conv_b,
           dt_bias, a_log, d_skip, ssm_norm_w, w_branch, w_out, final_norm_w):
    batch, seq, d = x.shape
    depth = norm_w.shape[0]
    width = ln_v_w.shape[1]
    n_heads = dt_bias.shape[1]
    conv_k, conv_dim = conv_w.shape[1:]
    assert width == d and n_heads * SSD_HEAD_DIM == width
    o_dt = 4 * width + conv_dim
    n_proj = o_dt + N_BRANCH * d
    t = batch * seq
    x2d = x.reshape(t, d)

    head_of_col = jnp.arange(width, dtype=jnp.int32) // SSD_HEAD_DIM
    e_rows = jnp.arange(LANES, dtype=jnp.int32)
    e3 = ((e_rows[:, None] % n_heads == head_of_col[None, :])
          & (e_rows[:, None] < SPLIT_TERMS * n_heads)).astype(BF16)
    tap = jnp.arange((conv_k - 1) * SSD_CHUNK, dtype=jnp.int32)
    src = SSD_CHUNK + tap % SSD_CHUNK - (conv_k - 1) + tap // SSD_CHUNK
    shift = (src[:, None] == jnp.arange(2 * SSD_CHUNK, dtype=jnp.int32)[None, :]).astype(BF16)

    for layer in range(depth):
        w_bf = w_in[layer].astype(BF16)
        w_dt = jnp.pad(w_bf[:, o_dt:o_dt + n_heads], ((0, 0), (0, LANES - n_heads)))
        w_all = lax.dynamic_update_slice(w_bf, w_bf[:, o_dt + n_heads:], (0, o_dt))
        proj, dt_raw = _inproj(x2d, norm_w[layer][None, :], w_all, w_dt, n_proj)

        y_a = _gmlp(proj, ln_v_w[layer][None, :], ln_v_b[layer][None, :],
                    w_spatial[layer], b_spatial[layer].T, width)

        pad_h = (0, LANES - n_heads)
        y_b = _ssd(proj, dt_raw, shift, conv_w[layer], conv_b[layer][None, :],
                   jnp.pad(dt_bias[layer], pad_h)[None, :], jnp.pad(a_log[layer], pad_h)[:, None],
                   jnp.repeat(d_skip[layer], SSD_HEAD_DIM)[None, :], ssm_norm_w[layer][None, :],
                   e3, batch, seq, width, n_heads)

        x2d = _merge(y_a, y_b, proj, x2d, b_gate[layer], w_branch[layer].astype(BF16),
                     w_out[layer].astype(BF16), final_norm_w[None, :], final=(layer == depth - 1))
    return x2d.reshape(batch, seq, d)
```

```python
import functools

import jax
import jax.numpy as jnp
from jax import lax
from jax.experimental import pallas as pl
from jax.experimental.pallas import tpu as pltpu

F32 = jnp.float32
BF16 = jnp.bfloat16

EPS = 1e-5
GMLP_GROUPS = 8
GMLP_CHUNK = 128
SSD_HEAD_DIM = 64
SSD_GROUPS = 8
SSD_STATE = 128
SSD_CHUNK = 128
N_BRANCH = 2

LANES = 128
SUBLANES = 8
SPLIT_TERMS = 3

CAST_TN = 1024
INPROJ_TM = 1024
INPROJ_TN = 2048
GMLP_TB = 512
MERGE_TM = 256
VMEM_LIMIT = 60000 * 1024

NT_DIMS = (((1,), (1,)), ((), ()))
TN_DIMS = (((0,), (0,)), ((), ()))


def _silu(x):
    hx = 0.5 * x
    return hx + hx * jnp.tanh(hx)


def _sigmoid(x):
    return 1.0 / (1.0 + jnp.exp(-x))


def _softplus(x):
    return jnp.maximum(x, 0.0) + jnp.log1p(jnp.exp(-jnp.abs(x)))


def _split_bf16(v, n_terms):
    terms = []
    r = v
    for _ in range(n_terms):
        t = r.astype(BF16).astype(F32)
        terms.append(t)
        r = r - t
    return terms


def _cast_kernel(w_ref, o_ref):
    o_ref[...] = w_ref[...].astype(o_ref.dtype)


def _cast_bf16(w):
    d, n = w.shape
    return pl.pallas_call(
        _cast_kernel,
        grid=(pl.cdiv(n, CAST_TN),),
        in_specs=[pl.BlockSpec((d, CAST_TN), lambda j: (0, j))],
        out_specs=pl.BlockSpec((d, CAST_TN), lambda j: (0, j)),
        out_shape=jax.ShapeDtypeStruct((d, n), BF16),
        compiler_params=pltpu.CompilerParams(
            dimension_semantics=("arbitrary",), vmem_limit_bytes=VMEM_LIMIT),
        name="wcast",
    )(w)


def _inproj_kernel(x_ref, nw_ref, w_ref, wdt_ref, proj_ref, dtt_ref, h_ref, *, row_chunk):
    j = pl.program_id(1)
    tm = x_ref.shape[0]

    @pl.when(j == 0)
    def _norm():
        def body(r, carry):
            r0 = pl.multiple_of(r * row_chunk, row_chunk)
            x = x_ref[pl.ds(r0, row_chunk), :]
            ms = jnp.mean(x * x, axis=-1, keepdims=True)
            h = x * lax.rsqrt(ms + EPS) * nw_ref[...]
            h_ref[pl.ds(r0, row_chunk), :] = h.astype(BF16)
            return carry
        lax.fori_loop(0, tm // row_chunk, body, 0)
        dtt_ref[...] = lax.dot_general(wdt_ref[...], h_ref[...], NT_DIMS,
                                       preferred_element_type=F32)

    proj_ref[...] = jnp.dot(h_ref[...], w_ref[...],
                            preferred_element_type=F32).astype(proj_ref.dtype)


def _inproj(x2d, norm_w, w_all, w_dt_t, n):
    t, d = x2d.shape
    n_heads = w_dt_t.shape[0]
    tm, tn = INPROJ_TM, INPROJ_TN
    assert t % tm == 0 and n % tn == 0 and n <= w_all.shape[1]
    return pl.pallas_call(
        functools.partial(_inproj_kernel, row_chunk=128),
        grid=(t // tm, n // tn),
        in_specs=[
            pl.BlockSpec((tm, d), lambda i, j: (i, 0)),
            pl.BlockSpec((1, d), lambda i, j: (0, 0)),
            pl.BlockSpec((d, tn), lambda i, j: (0, j)),
            pl.BlockSpec((n_heads, d), lambda i, j: (0, 0)),
        ],
        out_specs=[
            pl.BlockSpec((tm, tn), lambda i, j: (i, j)),
            pl.BlockSpec((n_heads, tm), lambda i, j: (0, i)),
        ],
        out_shape=[
            jax.ShapeDtypeStruct((t, n), BF16),
            jax.ShapeDtypeStruct((n_heads, t), F32),
        ],
        scratch_shapes=[pltpu.VMEM((tm, d), BF16)],
        compiler_params=pltpu.CompilerParams(
            dimension_semantics=("arbitrary", "arbitrary"),
            vmem_limit_bytes=VMEM_LIMIT),
        name="inproj",
    )(x2d, norm_w, w_all, w_dt_t)


def _gmlp_kernel(u_ref, v_ref, z_ref, lnw_ref, lnb_ref, ws_ref, bst_ref, y_ref):
    tb, e = u_ref.shape
    l = GMLP_CHUNK
    gd = e // GMLP_GROUPS
    row = lax.broadcasted_iota(jnp.int32, (l, l), 0)
    col = lax.broadcasted_iota(jnp.int32, (l, l), 1)
    causal = col <= row
    wm = [jnp.where(causal, ws_ref[g], 0.0).astype(BF16) for g in range(GMLP_GROUPS)]
    lnw = lnw_ref[...]
    lnb = lnb_ref[...]

    def chunk(ci, carry):
        r0 = pl.multiple_of(ci * l, l)
        v = v_ref[pl.ds(r0, l), :].astype(F32)
        mu = jnp.mean(v, axis=-1, keepdims=True)
        dv = v - mu
        var = jnp.mean(dv * dv, axis=-1, keepdims=True)
        vn = (dv * lax.rsqrt(var + EPS) * lnw + lnb).astype(BF16)
        for g in range(GMLP_GROUPS):
            cs = slice(g * gd, (g + 1) * gd)
            mixed = jnp.dot(wm[g], vn[:, cs], preferred_element_type=F32) + bst_ref[:, g:g + 1]
            gate = _silu(z_ref[pl.ds(r0, l), cs]) * u_ref[pl.ds(r0, l), cs]
            y_ref[pl.ds(r0, l), cs] = gate * mixed.astype(BF16)
        return carry

    lax.fori_loop(0, tb // l, chunk, 0)


def _gmlp(proj, ln_w, ln_b, w_s, b_s_t, width):
    t = proj.shape[0]
    tb = GMLP_TB
    g, l, _ = w_s.shape
    assert l == GMLP_CHUNK and g == GMLP_GROUPS and t % tb == 0
    return pl.pallas_call(
        _gmlp_kernel,
        grid=(t // tb,),
        in_specs=[
            pl.BlockSpec((tb, width), lambda i: (i, 0)),
            pl.BlockSpec((tb, width), lambda i: (i, 1)),
            pl.BlockSpec((tb, width), lambda i: (i, 2)),
            pl.BlockSpec((1, width), lambda i: (0, 0)),
            pl.BlockSpec((1, width), lambda i: (0, 0)),
            pl.BlockSpec((g, l, l), lambda i: (0, 0, 0)),
            pl.BlockSpec((l, g), lambda i: (0, 0)),
        ],
        out_specs=pl.BlockSpec((tb, width), lambda i: (i, 0)),
        out_shape=jax.ShapeDtypeStruct((t, width), BF16),
        compiler_params=pltpu.CompilerParams(
            dimension_semantics=("arbitrary",), vmem_limit_bytes=VMEM_LIMIT),
        name="gmlp",
    )(proj, proj, proj, ln_w, ln_b, w_s, b_s_t)


def _ssd_kernel(zb_ref, xbc_ref, dtt_ref, dtb_ref, alog_ref, shift_ref, cw_ref,
                cbias_ref, dskip_ref, nw_ref, e3_ref, y_ref, xx_ref, xc_ref, ht_ref, we_ref):
    c = pl.program_id(1)
    l = SSD_CHUNK
    n_heads = dtt_ref.shape[0]
    width = zb_ref.shape[1]
    conv_k = cw_ref.shape[0]
    n = SSD_STATE
    g_n = SSD_GROUPS
    gw = width // g_n
    hpg = gw // SSD_HEAD_DIM

    @pl.when(c == 0)
    def _init():
        xx_ref[0:l, :] = jnp.zeros((l, xx_ref.shape[1]), BF16)
        ht_ref[...] = jnp.zeros(ht_ref.shape, F32)

    cur = xbc_ref[...]
    xx_ref[l:2 * l, :] = cur
    shifted = jnp.dot(shift_ref[...], xx_ref[...], preferred_element_type=F32)
    acc = cbias_ref[...]
    for k in range(conv_k - 1):
        acc = acc + shifted[k * l:(k + 1) * l, :] * cw_ref[k:k + 1, :]
    acc = acc + cur.astype(F32) * cw_ref[conv_k - 1:conv_k, :]
    xc_ref[...] = _silu(acc)
    xx_ref[0:l, :] = cur

    row = lax.broadcasted_iota(jnp.int32, (l, l), 0)
    col = lax.broadcasted_iota(jnp.int32, (l, l), 1)
    causal = col <= row
    triu = (row <= col).astype(BF16)
    dt_t = _softplus(dtt_ref[...] + dtb_ref[...])
    adt_t = dt_t * (-jnp.exp(alog_ref[...]))
    a_row = None
    for term in _split_bf16(adt_t, SPLIT_TERMS):
        part = jnp.dot(term.astype(BF16), triu, preferred_element_type=F32)
        a_row = part if a_row is None else a_row + part
    a_last = a_row[:, l - 1:l]
    ap = a_row - jnp.log(dt_t)

    zpad = jnp.zeros((LANES - n_heads, l), F32)
    to_col = lambda v: jnp.concatenate([v, zpad], axis=0).T
    a_col = to_col(a_row)

    def pack(v):
        hi, mid, lo = _split_bf16(v, SPLIT_TERMS)
        return (hi + pltpu.roll(mid, n_heads, axis=1)
                + pltpu.roll(lo, 2 * n_heads, axis=1)).astype(BF16)

    e3 = e3_ref[...]
    we_ref[0:l, :] = jnp.dot(pack(to_col(dt_t * jnp.exp(a_last - a_row))), e3,
                             preferred_element_type=F32)
    we_ref[l:2 * l, :] = jnp.dot(pack(to_col(jnp.exp(a_row))), e3, preferred_element_type=F32)

    lane_g = lax.broadcasted_iota(jnp.int32, (l, gw), 1)
    for g in range(g_n):
        gs = slice(g * gw, (g + 1) * gw)
        b_g = xc_ref[:, width + g * n: width + (g + 1) * n].astype(BF16)
        c_g = xc_ref[:, width + (g_n + g) * n: width + (g_n + g + 1) * n].astype(BF16)
        cb = lax.dot_general(c_g, b_g, NT_DIMS, preferred_element_type=F32)
        cb = jnp.where(causal, cb, 0.0)
        xs_g = xc_ref[:, gs]
        xs_bf = xs_g.astype(BF16)

        scores = []
        xbd = []
        for k in range(hpg):
            h = g * hpg + k
            seg = a_col[:, h:h + 1] - ap[h:h + 1, :]
            scores.append((cb * jnp.exp(jnp.where(causal, seg, 0.0))).astype(BF16))
            in_head = (lane_g >= k * SSD_HEAD_DIM) & (lane_g < (k + 1) * SSD_HEAD_DIM)
            xbd.append(jnp.where(in_head, xs_bf, jnp.zeros_like(xs_bf)))
        s_cat = jnp.concatenate(scores, axis=1)
        x_bd = jnp.concatenate(xbd, axis=0)
        y_diag = jnp.dot(s_cat, x_bd, preferred_element_type=F32)

        h_prev = ht_ref[g]
        y_off = jnp.dot(c_g, h_prev.astype(BF16), preferred_element_type=F32) * we_ref[l:2 * l, gs]
        x_dec = (xs_g * we_ref[0:l, gs]).astype(BF16)
        st = lax.dot_general(b_g, x_dec, TN_DIMS, preferred_element_type=F32)
        ht_ref[g] = h_prev * we_ref[2 * l - 1:2 * l, gs] + st

        y = y_diag + y_off + dskip_ref[:, gs] * xs_g
        y = y * _silu(zb_ref[:, gs]).astype(F32)
        ms = jnp.mean(y * y, axis=-1, keepdims=True)
        y_ref[:, gs] = (y * lax.rsqrt(ms + EPS) * nw_ref[:, gs]).astype(y_ref.dtype)


def _ssd(proj, dt_t, dtb_col, alog_col, shift, conv_w, conv_b, dskip_e, norm_w, e3, batch, seq,
         width):
    t = proj.shape[0]
    n_heads = dt_t.shape[0]
    l = SSD_CHUNK
    nc = seq // l
    conv_k, conv_dim = conv_w.shape
    assert conv_dim == width + 2 * SSD_GROUPS * SSD_STATE and (4 * width) % conv_dim == 0
    assert width % SSD_GROUPS == 0 and conv_k - 1 <= l and SPLIT_TERMS * n_heads <= LANES
    assert shift.shape == ((conv_k - 1) * l, 2 * l)
    zb_blk = 3
    xbc_blk = (4 * width) // conv_dim
    tok = lambda b, c: b * nc + c
    const = lambda b, c: (0, 0)
    return pl.pallas_call(
        _ssd_kernel,
        grid=(batch, nc),
        in_specs=[
            pl.BlockSpec((l, width), lambda b, c: (tok(b, c), zb_blk)),
            pl.BlockSpec((l, conv_dim), lambda b, c: (tok(b, c), xbc_blk)),
            pl.BlockSpec((n_heads, l), lambda b, c: (0, tok(b, c))),
            pl.BlockSpec((n_heads, 1), const),
            pl.BlockSpec((n_heads, 1), const),
            pl.BlockSpec(shift.shape, const),
            pl.BlockSpec((conv_k, conv_dim), const),
            pl.BlockSpec((1, conv_dim), const),
            pl.BlockSpec((1, width), const),
            pl.BlockSpec((1, width), const),
            pl.BlockSpec((LANES, width), const),
        ],
        out_specs=pl.BlockSpec((l, width), lambda b, c: (tok(b, c), 0)),
        out_shape=jax.ShapeDtypeStruct((t, width), BF16),
        scratch_shapes=[
            pltpu.VMEM((2 * l, conv_dim), BF16),
            pltpu.VMEM((l, conv_dim), F32),
            pltpu.VMEM((SSD_GROUPS, SSD_STATE, width // SSD_GROUPS), F32),
            pltpu.VMEM((2 * l, width), F32),
        ],
        compiler_params=pltpu.CompilerParams(
            dimension_semantics=("arbitrary", "arbitrary"), vmem_limit_bytes=VMEM_LIMIT),
        name="ssd",
    )(proj, proj, dt_t, dtb_col, alog_col, shift, conv_w, conv_b, dskip_e, norm_w, e3)


def _merge_kernel(ya_ref, yb_ref, gl_ref, x_ref, bg_ref, wb_ref, wo_ref, fnw_ref, o_ref, *, final):
    d = x_ref.shape[1]
    bd0 = jnp.dot(ya_ref[...], wb_ref[0], preferred_element_type=F32)
    bd1 = jnp.dot(yb_ref[...], wb_ref[1], preferred_element_type=F32)
    g0 = _sigmoid(gl_ref[:, 0:d].astype(F32) + bg_ref[0:1, :])
    g1 = _sigmoid(gl_ref[:, d:2 * d].astype(F32) + bg_ref[1:2, :])
    merged = (g0 * bd0 + g1 * bd1).astype(BF16)
    xn = x_ref[...] + jnp.dot(merged, wo_ref[...], preferred_element_type=F32)
    if final:
        ms = jnp.mean(xn * xn, axis=-1, keepdims=True)
        xn = xn * lax.rsqrt(ms + EPS) * fnw_ref[...]
    o_ref[...] = xn


def _merge(y_a, y_b, proj, x2d, b_gate, w_branch, w_out, final_norm_w, final):
    t, d = x2d.shape
    e = y_a.shape[1]
    tm = MERGE_TM
    gate_blk = (proj.shape[1] - N_BRANCH * d) // (N_BRANCH * d)
    assert t % tm == 0 and gate_blk * N_BRANCH * d == proj.shape[1] - N_BRANCH * d
    resident = dict(pipeline_mode=pl.Buffered(1))
    return pl.pallas_call(
        functools.partial(_merge_kernel, final=final),
        grid=(t // tm,),
        in_specs=[
            pl.BlockSpec((tm, e), lambda i: (i, 0)),
            pl.BlockSpec((tm, e), lambda i: (i, 0)),
            pl.BlockSpec((tm, N_BRANCH * d), lambda i: (i, gate_blk)),
            pl.BlockSpec((tm, d), lambda i: (i, 0)),
            pl.BlockSpec((N_BRANCH, d), lambda i: (0, 0)),
            pl.BlockSpec((N_BRANCH, e, d), lambda i: (0, 0, 0), **resident),
            pl.BlockSpec((d, d), lambda i: (0, 0), **resident),
            pl.BlockSpec((1, d), lambda i: (0, 0)),
        ],
        out_specs=pl.BlockSpec((tm, d), lambda i: (i, 0)),
        out_shape=jax.ShapeDtypeStruct((t, d), F32),
        compiler_params=pltpu.CompilerParams(
            dimension_semantics=("arbitrary",), vmem_limit_bytes=VMEM_LIMIT),
        name="merge",
    )(y_a, y_b, proj, x2d, b_gate, w_branch, w_out, final_norm_w)


def kernel(x, norm_w, w_in, b_gate, ln_v_w, ln_v_b, w_spatial, b_spatial, conv_w, conv_b,
           dt_bias, a_log, d_skip, ssm_norm_w, w_branch, w_out, final_norm_w):
    batch, seq, d = x.shape
    depth = norm_w.shape[0]
    width = ln_v_w.shape[1]
    n_heads = dt_bias.shape[1]
    conv_k, conv_dim = conv_w.shape[1:]
    assert width == d and n_heads * SSD_HEAD_DIM == width
    o_dt = 4 * width + conv_dim
    n_proj = o_dt + N_BRANCH * d
    t = batch * seq
    x2d = x.reshape(t, d)

    head_of_col = jnp.arange(width, dtype=jnp.int32) // SSD_HEAD_DIM
    e_rows = jnp.arange(LANES, dtype=jnp.int32)
    e3 = ((e_rows[:, None] % n_heads == head_of_col[None, :])
          & (e_rows[:, None] < SPLIT_TERMS * n_heads)).astype(BF16)
    tap = jnp.arange((conv_k - 1) * SSD_CHUNK, dtype=jnp.int32)
    src = SSD_CHUNK + tap % SSD_CHUNK - (conv_k - 1) + tap // SSD_CHUNK
    shift = (src[:, None] == jnp.arange(2 * SSD_CHUNK, dtype=jnp.int32)[None, :]).astype(BF16)

    for layer in range(depth):
        w_bf = _cast_bf16(w_in[layer])
        w_dt_t = w_bf[:, o_dt:o_dt + n_heads].T
        w_all = lax.dynamic_update_slice(w_bf, w_bf[:, o_dt + n_heads:], (0, o_dt))
        proj, dt_t = _inproj(x2d, norm_w[layer][None, :], w_all, w_dt_t, n_proj)

        y_a = _gmlp(proj, ln_v_w[layer][None, :], ln_v_b[layer][None, :],
                    w_spatial[layer], b_spatial[layer].T, width)

        y_b = _ssd(proj, dt_t, dt_bias[layer][:, None], a_log[layer][:, None], shift,
                   conv_w[layer], conv_b[layer][None, :],
                   jnp.repeat(d_skip[layer], SSD_HEAD_DIM)[None, :], ssm_norm_w[layer][None, :],
                   e3, batch, seq, width)

        x2d = _merge(y_a, y_b, proj, x2d, b_gate[layer], w_branch[layer].astype(BF16),
                     w_out[layer].astype(BF16), final_norm_w[None, :], final=(layer == depth - 1))
    return x2d.reshape(batch, seq, d)
```

```python
import functools

import jax
import jax.numpy as jnp
from jax import lax
from jax.experimental import pallas as pl
from jax.experimental.pallas import tpu as pltpu

F32 = jnp.float32
BF16 = jnp.bfloat16

EPS = 1e-5
GMLP_GROUPS = 8
GMLP_CHUNK = 128
SSD_HEAD_DIM = 64
SSD_GROUPS = 8
SSD_STATE = 128
SSD_CHUNK = 128
N_BRANCH = 2

LANES = 128
SUBLANES = 8
SPLIT_TERMS = 3

CAST_TR = 1024
INPROJ_TM = 1024
INPROJ_TN = 2048
GMLP_TB = 512
MERGE_TM = 256
VMEM_LIMIT = 60000 * 1024

NT_DIMS = (((1,), (1,)), ((), ()))
TN_DIMS = (((0,), (0,)), ((), ()))


def _silu(x):
    hx = 0.5 * x
    return hx + hx * jnp.tanh(hx)


def _sigmoid(x):
    return 1.0 / (1.0 + jnp.exp(-x))


def _softplus(x):
    return jnp.maximum(x, 0.0) + jnp.log1p(jnp.exp(-jnp.abs(x)))


def _split_bf16(v, n_terms):
    terms = []
    r = v
    for _ in range(n_terms):
        t = r.astype(BF16).astype(F32)
        terms.append(t)
        r = r - t
    return terms


def _cast_kernel(w_ref, o_ref):
    o_ref[...] = w_ref[...].astype(o_ref.dtype)


def _cast_bf16(w):
    n, d = w.shape
    return pl.pallas_call(
        _cast_kernel,
        grid=(pl.cdiv(n, CAST_TR),),
        in_specs=[pl.BlockSpec((CAST_TR, d), lambda j: (j, 0))],
        out_specs=pl.BlockSpec((CAST_TR, d), lambda j: (j, 0)),
        out_shape=jax.ShapeDtypeStruct((n, d), BF16),
        compiler_params=pltpu.CompilerParams(
            dimension_semantics=("arbitrary",), vmem_limit_bytes=VMEM_LIMIT),
        name="wcast",
    )(w)


def _inproj_kernel(x_ref, nw_ref, w_ref, wdt_ref, proj_ref, dtt_ref, h_ref, *, row_chunk):
    j = pl.program_id(1)
    tm = x_ref.shape[0]

    @pl.when(j == 0)
    def _norm():
        def body(r, carry):
            r0 = pl.multiple_of(r * row_chunk, row_chunk)
            x = x_ref[pl.ds(r0, row_chunk), :]
            ms = jnp.mean(x * x, axis=-1, keepdims=True)
            h = x * lax.rsqrt(ms + EPS) * nw_ref[...]
            h_ref[pl.ds(r0, row_chunk), :] = h.astype(BF16)
            return carry
        lax.fori_loop(0, tm // row_chunk, body, 0)
        dtt_ref[...] = lax.dot_general(wdt_ref[...], h_ref[...], NT_DIMS,
                                       preferred_element_type=F32)

    proj_ref[...] = lax.dot_general(h_ref[...], w_ref[...], NT_DIMS,
                                    preferred_element_type=F32).astype(proj_ref.dtype)


def _inproj(x2d, norm_w, w_t, o_dt, n_heads):
    t, d = x2d.shape
    n = w_t.shape[0] - n_heads
    tm, tn = INPROJ_TM, INPROJ_TN
    assert t % tm == 0 and n % tn == 0 and o_dt % tn == 0 and o_dt % n_heads == 0
    w_row = lambda i, j: (
        pl.multiple_of(j * tn + jnp.where(j * tn >= o_dt, n_heads, 0), n_heads), 0)
    return pl.pallas_call(
        functools.partial(_inproj_kernel, row_chunk=128),
        grid=(t // tm, n // tn),
        in_specs=[
            pl.BlockSpec((tm, d), lambda i, j: (i, 0)),
            pl.BlockSpec((1, d), lambda i, j: (0, 0)),
            pl.BlockSpec((pl.Element(tn), pl.Element(d)), w_row),
            pl.BlockSpec((n_heads, d), lambda i, j: (o_dt // n_heads, 0)),
        ],
        out_specs=[
            pl.BlockSpec((tm, tn), lambda i, j: (i, j)),
            pl.BlockSpec((n_heads, tm), lambda i, j: (0, i)),
        ],
        out_shape=[
            jax.ShapeDtypeStruct((t, n), BF16),
            jax.ShapeDtypeStruct((n_heads, t), F32),
        ],
        scratch_shapes=[pltpu.VMEM((tm, d), BF16)],
        compiler_params=pltpu.CompilerParams(
            dimension_semantics=("arbitrary", "arbitrary"),
            vmem_limit_bytes=VMEM_LIMIT),
        name="inproj",
    )(x2d, norm_w, w_t, w_t)


def _gmlp_kernel(u_ref, v_ref, z_ref, lnw_ref, lnb_ref, ws_ref, bst_ref, y_ref):
    tb, e = u_ref.shape
    l = GMLP_CHUNK
    gd = e // GMLP_GROUPS
    row = lax.broadcasted_iota(jnp.int32, (l, l), 0)
    col = lax.broadcasted_iota(jnp.int32, (l, l), 1)
    causal = col <= row
    wm = [jnp.where(causal, ws_ref[g], 0.0).astype(BF16) for g in range(GMLP_GROUPS)]
    lnw = lnw_ref[...]
    lnb = lnb_ref[...]

    def chunk(ci, carry):
        r0 = pl.multiple_of(ci * l, l)
        v = v_ref[pl.ds(r0, l), :].astype(F32)
        mu = jnp.mean(v, axis=-1, keepdims=True)
        dv = v - mu
        var = jnp.mean(dv * dv, axis=-1, keepdims=True)
        vn = (dv * lax.rsqrt(var + EPS) * lnw + lnb).astype(BF16)
        for g in range(GMLP_GROUPS):
            cs = slice(g * gd, (g + 1) * gd)
            mixed = jnp.dot(wm[g], vn[:, cs], preferred_element_type=F32) + bst_ref[:, g:g + 1]
            gate = _silu(z_ref[pl.ds(r0, l), cs]) * u_ref[pl.ds(r0, l), cs]
            y_ref[pl.ds(r0, l), cs] = gate * mixed.astype(BF16)
        return carry

    lax.fori_loop(0, tb // l, chunk, 0)


def _gmlp(proj, ln_w, ln_b, w_s, b_s_t, width):
    t = proj.shape[0]
    tb = GMLP_TB
    g, l, _ = w_s.shape
    assert l == GMLP_CHUNK and g == GMLP_GROUPS and t % tb == 0
    return pl.pallas_call(
        _gmlp_kernel,
        grid=(t // tb,),
        in_specs=[
            pl.BlockSpec((tb, width), lambda i: (i, 0)),
            pl.BlockSpec((tb, width), lambda i: (i, 1)),
            pl.BlockSpec((tb, width), lambda i: (i, 2)),
            pl.BlockSpec((1, width), lambda i: (0, 0)),
            pl.BlockSpec((1, width), lambda i: (0, 0)),
            pl.BlockSpec((g, l, l), lambda i: (0, 0, 0)),
            pl.BlockSpec((l, g), lambda i: (0, 0)),
        ],
        out_specs=pl.BlockSpec((tb, width), lambda i: (i, 0)),
        out_shape=jax.ShapeDtypeStruct((t, width), BF16),
        compiler_params=pltpu.CompilerParams(
            dimension_semantics=("arbitrary",), vmem_limit_bytes=VMEM_LIMIT),
        name="gmlp",
    )(proj, proj, proj, ln_w, ln_b, w_s, b_s_t)


def _ssd_kernel(zb_ref, xbc_ref, dtt_ref, dtb_ref, alog_ref, shift_ref, cw_ref,
                cbias_ref, dskip_ref, nw_ref, e3_ref, y_ref, xx_ref, xc_ref, ht_ref, we_ref):
    c = pl.program_id(1)
    l = SSD_CHUNK
    n_heads = dtt_ref.shape[0]
    width = zb_ref.shape[1]
    conv_k = cw_ref.shape[0]
    n = SSD_STATE
    g_n = SSD_GROUPS
    gw = width // g_n
    hpg = gw // SSD_HEAD_DIM

    @pl.when(c == 0)
    def _init():
        xx_ref[0:l, :] = jnp.zeros((l, xx_ref.shape[1]), BF16)
        ht_ref[...] = jnp.zeros(ht_ref.shape, F32)

    cur = xbc_ref[...]
    xx_ref[l:2 * l, :] = cur
    shifted = jnp.dot(shift_ref[...], xx_ref[...], preferred_element_type=F32)
    acc = cbias_ref[...]
    for k in range(conv_k - 1):
        acc = acc + shifted[k * l:(k + 1) * l, :] * cw_ref[k:k + 1, :]
    acc = acc + cur.astype(F32) * cw_ref[conv_k - 1:conv_k, :]
    xc_ref[...] = _silu(acc)
    xx_ref[0:l, :] = cur

    row = lax.broadcasted_iota(jnp.int32, (l, l), 0)
    col = lax.broadcasted_iota(jnp.int32, (l, l), 1)
    causal = col <= row
    triu = (row <= col).astype(BF16)
    dt_t = _softplus(dtt_ref[...] + dtb_ref[...])
    adt_t = dt_t * (-jnp.exp(alog_ref[...]))
    a_row = None
    for term in _split_bf16(adt_t, SPLIT_TERMS):
        part = jnp.dot(term.astype(BF16), triu, preferred_element_type=F32)
        a_row = part if a_row is None else a_row + part
    a_last = a_row[:, l - 1:l]
    ap = a_row - jnp.log(dt_t)

    zpad = jnp.zeros((LANES - n_heads, l), F32)
    to_col = lambda v: jnp.concatenate([v, zpad], axis=0).T
    a_col = to_col(a_row)

    def pack(v):
        hi, mid, lo = _split_bf16(v, SPLIT_TERMS)
        return (hi + pltpu.roll(mid, n_heads, axis=1)
                + pltpu.roll(lo, 2 * n_heads, axis=1)).astype(BF16)

    e3 = e3_ref[...]
    we_ref[0:l, :] = jnp.dot(pack(to_col(dt_t * jnp.exp(a_last - a_row))), e3,
                             preferred_element_type=F32)
    we_ref[l:2 * l, :] = jnp.dot(pack(to_col(jnp.exp(a_row))), e3, preferred_element_type=F32)

    lane_g = lax.broadcasted_iota(jnp.int32, (l, gw), 1)
    for g in range(g_n):
        gs = slice(g * gw, (g + 1) * gw)
        b_g = xc_ref[:, width + g * n: width + (g + 1) * n].astype(BF16)
        c_g = xc_ref[:, width + (g_n + g) * n: width + (g_n + g + 1) * n].astype(BF16)
        cb = lax.dot_general(c_g, b_g, NT_DIMS, preferred_element_type=F32)
        cb = jnp.where(causal, cb, 0.0)
        xs_g = xc_ref[:, gs]
        xs_bf = xs_g.astype(BF16)

        scores = []
        xbd = []
        for k in range(hpg):
            h = g * hpg + k
            seg = a_col[:, h:h + 1] - ap[h:h + 1, :]
            scores.append((cb * jnp.exp(jnp.where(causal, seg, 0.0))).astype(BF16))
            in_head = (lane_g >= k * SSD_HEAD_DIM) & (lane_g < (k + 1) * SSD_HEAD_DIM)
            xbd.append(jnp.where(in_head, xs_bf, jnp.zeros_like(xs_bf)))
        s_cat = jnp.concatenate(scores, axis=1)
        x_bd = jnp.concatenate(xbd, axis=0)
        y_diag = jnp.dot(s_cat, x_bd, preferred_element_type=F32)

        h_prev = ht_ref[g]
        y_off = jnp.dot(c_g, h_prev.astype(BF16), preferred_element_type=F32) * we_ref[l:2 * l, gs]
        x_dec = (xs_g * we_ref[0:l, gs]).astype(BF16)
        st = lax.dot_general(b_g, x_dec, TN_DIMS, preferred_element_type=F32)
        ht_ref[g] = h_prev * we_ref[2 * l - 1:2 * l, gs] + st

        y = y_diag + y_off + dskip_ref[:, gs] * xs_g
        y = y * _silu(zb_ref[:, gs]).astype(F32)
        ms = jnp.mean(y * y, axis=-1, keepdims=True)
        y_ref[:, gs] = (y * lax.rsqrt(ms + EPS) * nw_ref[:, gs]).astype(y_ref.dtype)


def _ssd(proj, dt_t, dtb_col, alog_col, shift, conv_w, conv_b, dskip_e, norm_w, e3, batch, seq,
         width):
    t = proj.shape[0]
    n_heads = dt_t.shape[0]
    l = SSD_CHUNK
    nc = seq // l
    conv_k, conv_dim = conv_w.shape
    assert conv_dim == width + 2 * SSD_GROUPS * SSD_STATE and (4 * width) % conv_dim == 0
    assert width % SSD_GROUPS == 0 and conv_k - 1 <= l and SPLIT_TERMS * n_heads <= LANES
    assert shift.shape == ((conv_k - 1) * l, 2 * l)
    zb_blk = 3
    xbc_blk = (4 * width) // conv_dim
    tok = lambda b, c: b * nc + c
    const = lambda b, c: (0, 0)
    return pl.pallas_call(
        _ssd_kernel,
        grid=(batch, nc),
        in_specs=[
            pl.BlockSpec((l, width), lambda b, c: (tok(b, c), zb_blk)),
            pl.BlockSpec((l, conv_dim), lambda b, c: (tok(b, c), xbc_blk)),
            pl.BlockSpec((n_heads, l), lambda b, c: (0, tok(b, c))),
            pl.BlockSpec((n_heads, 1), const),
            pl.BlockSpec((n_heads, 1), const),
            pl.BlockSpec(shift.shape, const),
            pl.BlockSpec((conv_k, conv_dim), const),
            pl.BlockSpec((1, conv_dim), const),
            pl.BlockSpec((1, width), const),
            pl.BlockSpec((1, width), const),
            pl.BlockSpec((LANES, width), const),
        ],
        out_specs=pl.BlockSpec((l, width), lambda b, c: (tok(b, c), 0)),
        out_shape=jax.ShapeDtypeStruct((t, width), BF16),
        scratch_shapes=[
            pltpu.VMEM((2 * l, conv_dim), BF16),
            pltpu.VMEM((l, conv_dim), F32),
            pltpu.VMEM((SSD_GROUPS, SSD_STATE, width // SSD_GROUPS), F32),
            pltpu.VMEM((2 * l, width), F32),
        ],
        compiler_params=pltpu.CompilerParams(
            dimension_semantics=("arbitrary", "arbitrary"), vmem_limit_bytes=VMEM_LIMIT),
        name="ssd",
    )(proj, proj, dt_t, dtb_col, alog_col, shift, conv_w, conv_b, dskip_e, norm_w, e3)


def _merge_kernel(ya_ref, yb_ref, gl_ref, x_ref, bg_ref, wb_ref, wo_ref, fnw_ref, o_ref, *, final):
    d = x_ref.shape[1]
    bd0 = jnp.dot(ya_ref[...], wb_ref[0], preferred_element_type=F32)
    bd1 = jnp.dot(yb_ref[...], wb_ref[1], preferred_element_type=F32)
    g0 = _sigmoid(gl_ref[:, 0:d].astype(F32) + bg_ref[0:1, :])
    g1 = _sigmoid(gl_ref[:, d:2 * d].astype(F32) + bg_ref[1:2, :])
    merged = (g0 * bd0 + g1 * bd1).astype(BF16)
    xn = x_ref[...] + jnp.dot(merged, wo_ref[...], preferred_element_type=F32)
    if final:
        ms = jnp.mean(xn * xn, axis=-1, keepdims=True)
        xn = xn * lax.rsqrt(ms + EPS) * fnw_ref[...]
    o_ref[...] = xn


def _merge(y_a, y_b, proj, x2d, b_gate, w_branch, w_out, final_norm_w, final):
    t, d = x2d.shape
    e = y_a.shape[1]
    tm = MERGE_TM
    gate_blk = (proj.shape[1] - N_BRANCH * d) // (N_BRANCH * d)
    assert t % tm == 0 and gate_blk * N_BRANCH * d == proj.shape[1] - N_BRANCH * d
    resident = dict(pipeline_mode=pl.Buffered(1))
    return pl.pallas_call(
        functools.partial(_merge_kernel, final=final),
        grid=(t // tm,),
        in_specs=[
            pl.BlockSpec((tm, e), lambda i: (i, 0)),
            pl.BlockSpec((tm, e), lambda i: (i, 0)),
            pl.BlockSpec((tm, N_BRANCH * d), lambda i: (i, gate_blk)),
            pl.BlockSpec((tm, d), lambda i: (i, 0)),
            pl.BlockSpec((N_BRANCH, d), lambda i: (0, 0)),
            pl.BlockSpec((N_BRANCH, e, d), lambda i: (0, 0, 0), **resident),
            pl.BlockSpec((d, d), lambda i: (0, 0), **resident),
            pl.BlockSpec((1, d), lambda i: (0, 0)),
        ],
        out_specs=pl.BlockSpec((tm, d), lambda i: (i, 0)),
        out_shape=jax.ShapeDtypeStruct((t, d), F32),
        compiler_params=pltpu.CompilerParams(
            dimension_semantics=("arbitrary",), vmem_limit_bytes=VMEM_LIMIT),
        name="merge",
    )(y_a, y_b, proj, x2d, b_gate, w_branch, w_out, final_norm_w)


def kernel(x, norm_w, w_in, b_gate, ln_v_w, ln_v_b, w_spatial, b_spatial, conv_w, conv_b,
           dt_bias, a_log, d_skip, ssm_norm_w, w_branch, w_out, final_norm_w):
    batch, seq, d = x.shape
    depth = norm_w.shape[0]
    width = ln_v_w.shape[1]
    n_heads = dt_bias.shape[1]
    conv_k, conv_dim = conv_w.shape[1:]
    assert width == d and n_heads * SSD_HEAD_DIM == width
    o_dt = 4 * width + conv_dim
    n_proj = o_dt + N_BRANCH * d
    t = batch * seq
    x2d = x.reshape(t, d)

    head_of_col = jnp.arange(width, dtype=jnp.int32) // SSD_HEAD_DIM
    e_rows = jnp.arange(LANES, dtype=jnp.int32)
    e3 = ((e_rows[:, None] % n_heads == head_of_col[None, :])
          & (e_rows[:, None] < SPLIT_TERMS * n_heads)).astype(BF16)
    tap = jnp.arange((conv_k - 1) * SSD_CHUNK, dtype=jnp.int32)
    src = SSD_CHUNK + tap % SSD_CHUNK - (conv_k - 1) + tap // SSD_CHUNK
    shift = (src[:, None] == jnp.arange(2 * SSD_CHUNK, dtype=jnp.int32)[None, :]).astype(BF16)

    for layer in range(depth):
        w_t = _cast_bf16(jnp.swapaxes(w_in[layer], 0, 1))
        proj, dt_t = _inproj(x2d, norm_w[layer][None, :], w_t, o_dt, n_heads)

        y_a = _gmlp(proj, ln_v_w[layer][None, :], ln_v_b[layer][None, :],
                    w_spatial[layer], b_spatial[layer].T, width)

        y_b = _ssd(proj, dt_t, dt_bias[layer][:, None], a_log[layer][:, None], shift,
                   conv_w[layer], conv_b[layer][None, :],
                   jnp.repeat(d_skip[layer], SSD_HEAD_DIM)[None, :], ssm_norm_w[layer][None, :],
                   e3, batch, seq, width)

        x2d = _merge(y_a, y_b, proj, x2d, b_gate[layer], w_branch[layer].astype(BF16),
                     w_out[layer].astype(BF16), final_norm_w[None, :], final=(layer == depth - 1))
    return x2d.reshape(batch, seq, d)
```

```python
import functools

import jax
import jax.numpy as jnp
from jax import lax
from jax.experimental import pallas as pl
from jax.experimental.pallas import tpu as pltpu

F32 = jnp.float32
BF16 = jnp.bfloat16

EPS = 1e-5
GMLP_GROUPS = 8
GMLP_CHUNK = 128
SSD_HEAD_DIM = 64
SSD_GROUPS = 8
SSD_STATE = 128
SSD_CHUNK = 128
N_BRANCH = 2

LANES = 128
SUBLANES = 8
SPLIT_TERMS = 3
CONV_TAIL = 2 * SUBLANES

CAST_TR = 1024
INPROJ_TM = 1024
INPROJ_TN = 2048
GMLP_TB = 1024
SSD_TB = 512
MERGE_TM = 256
VMEM_LIMIT = 60000 * 1024

NT_DIMS = (((1,), (1,)), ((), ()))
TN_DIMS = (((0,), (0,)), ((), ()))


def _silu(x):
    hx = 0.5 * x
    return hx + hx * jnp.tanh(hx)


def _sigmoid(x):
    return 1.0 / (1.0 + jnp.exp(-x))


def _softplus(x):
    return jnp.maximum(x, 0.0) + jnp.log1p(jnp.exp(-jnp.abs(x)))


def _split_bf16(v, n_terms):
    terms = []
    r = v
    for _ in range(n_terms):
        t = r.astype(BF16).astype(F32)
        terms.append(t)
        r = r - t
    return terms


def _cast_kernel(w_ref, o_ref):
    o_ref[...] = w_ref[...].astype(o_ref.dtype)


def _cast_bf16(w):
    n, d = w.shape
    return pl.pallas_call(
        _cast_kernel,
        grid=(pl.cdiv(n, CAST_TR),),
        in_specs=[pl.BlockSpec((CAST_TR, d), lambda j: (j, 0))],
        out_specs=pl.BlockSpec((CAST_TR, d), lambda j: (j, 0)),
        out_shape=jax.ShapeDtypeStruct((n, d), BF16),
        compiler_params=pltpu.CompilerParams(
            dimension_semantics=("arbitrary",), vmem_limit_bytes=VMEM_LIMIT),
        name="wcast",
    )(w)


def _inproj_kernel(x_ref, nw_ref, w_ref, wdt_ref, proj_ref, dtt_ref, h_ref, *, row_chunk):
    j = pl.program_id(1)
    tm = x_ref.shape[0]

    @pl.when(j == 0)
    def _norm():
        def body(r, carry):
            r0 = pl.multiple_of(r * row_chunk, row_chunk)
            x = x_ref[pl.ds(r0, row_chunk), :]
            ms = jnp.mean(x * x, axis=-1, keepdims=True)
            h = x * lax.rsqrt(ms + EPS) * nw_ref[...]
            h_ref[pl.ds(r0, row_chunk), :] = h.astype(BF16)
            return carry
        lax.fori_loop(0, tm // row_chunk, body, 0)
        dtt_ref[...] = lax.dot_general(wdt_ref[...], h_ref[...], NT_DIMS,
                                       preferred_element_type=F32)

    proj_ref[...] = lax.dot_general(h_ref[...], w_ref[...], NT_DIMS,
                                    preferred_element_type=F32).astype(proj_ref.dtype)


def _inproj(x2d, norm_w, w_t, o_dt, n_heads):
    t, d = x2d.shape
    n = w_t.shape[0] - n_heads
    tm, tn = INPROJ_TM, INPROJ_TN
    assert t % tm == 0 and n % tn == 0 and o_dt % tn == 0 and o_dt % n_heads == 0
    w_row = lambda i, j: (
        pl.multiple_of(j * tn + jnp.where(j * tn >= o_dt, n_heads, 0), n_heads), 0)
    return pl.pallas_call(
        functools.partial(_inproj_kernel, row_chunk=128),
        grid=(t // tm, n // tn),
        in_specs=[
            pl.BlockSpec((tm, d), lambda i, j: (i, 0)),
            pl.BlockSpec((1, d), lambda i, j: (0, 0)),
            pl.BlockSpec((pl.Element(tn), pl.Element(d)), w_row),
            pl.BlockSpec((n_heads, d), lambda i, j: (o_dt // n_heads, 0)),
        ],
        out_specs=[
            pl.BlockSpec((tm, tn), lambda i, j: (i, j)),
            pl.BlockSpec((n_heads, tm), lambda i, j: (0, i)),
        ],
        out_shape=[
            jax.ShapeDtypeStruct((t, n), BF16),
            jax.ShapeDtypeStruct((n_heads, t), F32),
        ],
        scratch_shapes=[pltpu.VMEM((tm, d), BF16)],
        compiler_params=pltpu.CompilerParams(
            dimension_semantics=("arbitrary", "arbitrary"),
            vmem_limit_bytes=VMEM_LIMIT),
        name="inproj",
    )(x2d, norm_w, w_t, w_t)


def _gmlp_kernel(u_ref, v_ref, z_ref, lnw_ref, lnb_ref, ws_ref, bst_ref, y_ref):
    tb, e = u_ref.shape
    l = GMLP_CHUNK
    gd = e // GMLP_GROUPS
    row = lax.broadcasted_iota(jnp.int32, (l, l), 0)
    col = lax.broadcasted_iota(jnp.int32, (l, l), 1)
    causal = col <= row
    wm = [jnp.where(causal, ws_ref[g], 0.0).astype(BF16) for g in range(GMLP_GROUPS)]
    lnw = lnw_ref[...]
    lnb = lnb_ref[...]

    def chunk(ci, carry):
        r0 = pl.multiple_of(ci * l, l)
        v = v_ref[pl.ds(r0, l), :].astype(F32)
        mu = jnp.mean(v, axis=-1, keepdims=True)
        dv = v - mu
        var = jnp.mean(dv * dv, axis=-1, keepdims=True)
        vn = (dv * lax.rsqrt(var + EPS) * lnw + lnb).astype(BF16)
        for g in range(GMLP_GROUPS):
            cs = slice(g * gd, (g + 1) * gd)
            mixed = jnp.dot(wm[g], vn[:, cs], preferred_element_type=F32) + bst_ref[:, g:g + 1]
            gate = _silu(z_ref[pl.ds(r0, l), cs]) * u_ref[pl.ds(r0, l), cs]
            y_ref[pl.ds(r0, l), cs] = gate * mixed.astype(BF16)
        return carry

    lax.fori_loop(0, tb // l, chunk, 0)


def _gmlp(proj, ln_w, ln_b, w_s, b_s_t, width):
    t = proj.shape[0]
    tb = GMLP_TB
    g, l, _ = w_s.shape
    assert l == GMLP_CHUNK and g == GMLP_GROUPS and t % tb == 0
    return pl.pallas_call(
        _gmlp_kernel,
        grid=(t // tb,),
        in_specs=[
            pl.BlockSpec((tb, width), lambda i: (i, 0)),
            pl.BlockSpec((tb, width), lambda i: (i, 1)),
            pl.BlockSpec((tb, width), lambda i: (i, 2)),
            pl.BlockSpec((1, width), lambda i: (0, 0)),
            pl.BlockSpec((1, width), lambda i: (0, 0)),
            pl.BlockSpec((g, l, l), lambda i: (0, 0, 0)),
            pl.BlockSpec((l, g), lambda i: (0, 0)),
        ],
        out_specs=pl.BlockSpec((tb, width), lambda i: (i, 0)),
        out_shape=jax.ShapeDtypeStruct((t, width), BF16),
        compiler_params=pltpu.CompilerParams(
            dimension_semantics=("arbitrary",), vmem_limit_bytes=VMEM_LIMIT),
        name="gmlp",
    )(proj, proj, proj, ln_w, ln_b, w_s, b_s_t)


def _ssd_kernel(zb_ref, xbc_ref, dtt_ref, dtb_ref, alog_ref, shift_ref, cw_ref,
                cbias_ref, dskip_ref, nw_ref, e3_ref, y_ref, xx_ref, xc_ref, ht_ref, we_ref):
    c = pl.program_id(1)
    l = SSD_CHUNK
    n_heads = dtt_ref.shape[0]
    width = zb_ref.shape[1]
    conv_k = cw_ref.shape[0]
    n = SSD_STATE
    g_n = SSD_GROUPS
    gw = width // g_n
    hpg = gw // SSD_HEAD_DIM
    tail = xx_ref.shape[0] - l

    @pl.when(c == 0)
    def _init():
        xx_ref[0:tail, :] = jnp.zeros((tail, xx_ref.shape[1]), BF16)
        ht_ref[...] = jnp.zeros(ht_ref.shape, F32)

    def chunk(ci, carry):
        rows = pl.ds(pl.multiple_of(ci * l, l), l)
        cur = xbc_ref[rows, :]
        xx_ref[tail:tail + l, :] = cur
        shifted = jnp.dot(shift_ref[...], xx_ref[...], preferred_element_type=F32)
        acc = cbias_ref[...]
        for k in range(conv_k - 1):
            acc = acc + shifted[k * l:(k + 1) * l, :] * cw_ref[k:k + 1, :]
        acc = acc + cur.astype(F32) * cw_ref[conv_k - 1:conv_k, :]
        xc_ref[...] = _silu(acc)
        xx_ref[0:tail, :] = cur[l - tail:l, :]

        row = lax.broadcasted_iota(jnp.int32, (l, l), 0)
        col = lax.broadcasted_iota(jnp.int32, (l, l), 1)
        causal = col <= row
        triu = (row <= col).astype(BF16)
        dt_t = _softplus(dtt_ref[:, rows] + dtb_ref[...])
        adt_t = dt_t * (-jnp.exp(alog_ref[...]))
        a_row = None
        for term in _split_bf16(adt_t, SPLIT_TERMS):
            part = jnp.dot(term.astype(BF16), triu, preferred_element_type=F32)
            a_row = part if a_row is None else a_row + part
        a_last = a_row[:, l - 1:l]
        ap = a_row - jnp.log(dt_t)

        zpad = jnp.zeros((LANES - n_heads, l), F32)
        to_col = lambda v: jnp.concatenate([v, zpad], axis=0).T
        a_col = to_col(a_row)

        def pack(v):
            hi, mid, lo = _split_bf16(v, SPLIT_TERMS)
            return (hi + pltpu.roll(mid, n_heads, axis=1)
                    + pltpu.roll(lo, 2 * n_heads, axis=1)).astype(BF16)

        e3 = e3_ref[...]
        we_ref[0:l, :] = jnp.dot(pack(to_col(dt_t * jnp.exp(a_last - a_row))), e3,
                                 preferred_element_type=F32)
        we_ref[l:2 * l, :] = jnp.dot(pack(to_col(jnp.exp(a_row))), e3, preferred_element_type=F32)

        lane_g = lax.broadcasted_iota(jnp.int32, (l, gw), 1)
        for g in range(g_n):
            gs = slice(g * gw, (g + 1) * gw)
            b_g = xc_ref[:, width + g * n: width + (g + 1) * n].astype(BF16)
            c_g = xc_ref[:, width + (g_n + g) * n: width + (g_n + g + 1) * n].astype(BF16)
            cb = lax.dot_general(c_g, b_g, NT_DIMS, preferred_element_type=F32)
            cb = jnp.where(causal, cb, 0.0)
            xs_g = xc_ref[:, gs]
            xs_bf = xs_g.astype(BF16)

            scores = []
            xbd = []
            for k in range(hpg):
                h = g * hpg + k
                seg = a_col[:, h:h + 1] - ap[h:h + 1, :]
                scores.append((cb * jnp.exp(jnp.where(causal, seg, 0.0))).astype(BF16))
                in_head = (lane_g >= k * SSD_HEAD_DIM) & (lane_g < (k + 1) * SSD_HEAD_DIM)
                xbd.append(jnp.where(in_head, xs_bf, jnp.zeros_like(xs_bf)))
            s_cat = jnp.concatenate(scores, axis=1)
            x_bd = jnp.concatenate(xbd, axis=0)
            y_diag = jnp.dot(s_cat, x_bd, preferred_element_type=F32)

            h_prev = ht_ref[g]
            y_off = jnp.dot(c_g, h_prev.astype(BF16), preferred_element_type=F32) * we_ref[l:2 * l, gs]
            x_dec = (xs_g * we_ref[0:l, gs]).astype(BF16)
            st = lax.dot_general(b_g, x_dec, TN_DIMS, preferred_element_type=F32)
            ht_ref[g] = h_prev * we_ref[2 * l - 1:2 * l, gs] + st

            y = y_diag + y_off + dskip_ref[:, gs] * xs_g
            y = y * _silu(zb_ref[rows, gs]).astype(F32)
            ms = jnp.mean(y * y, axis=-1, keepdims=True)
            y_ref[rows, gs] = (y * lax.rsqrt(ms + EPS) * nw_ref[:, gs]).astype(y_ref.dtype)
        return carry

    lax.fori_loop(0, zb_ref.shape[0] // l, chunk, 0)


def _ssd(proj, dt_t, dtb_col, alog_col, shift, conv_w, conv_b, dskip_e, norm_w, e3, batch, seq,
         width):
    t = proj.shape[0]
    n_heads = dt_t.shape[0]
    l = SSD_CHUNK
    conv_k, conv_dim = conv_w.shape
    assert conv_dim == width + 2 * SSD_GROUPS * SSD_STATE and (4 * width) % conv_dim == 0
    assert width % SSD_GROUPS == 0 and conv_k - 1 <= l and SPLIT_TERMS * n_heads <= LANES
    assert shift.shape == ((conv_k - 1) * l, CONV_TAIL + l) and conv_k - 1 <= CONV_TAIL
    zb_blk = 3
    xbc_blk = (4 * width) // conv_dim
    tb = SSD_TB
    nb = seq // tb
    assert seq % tb == 0 and tb % l == 0
    tok = lambda b, c: b * nb + c
    const = lambda b, c: (0, 0)
    return pl.pallas_call(
        _ssd_kernel,
        grid=(batch, nb),
        in_specs=[
            pl.BlockSpec((tb, width), lambda b, c: (tok(b, c), zb_blk)),
            pl.BlockSpec((tb, conv_dim), lambda b, c: (tok(b, c), xbc_blk)),
            pl.BlockSpec((n_heads, tb), lambda b, c: (0, tok(b, c))),
            pl.BlockSpec((n_heads, 1), const),
            pl.BlockSpec((n_heads, 1), const),
            pl.BlockSpec(shift.shape, const),
            pl.BlockSpec((conv_k, conv_dim), const),
            pl.BlockSpec((1, conv_dim), const),
            pl.BlockSpec((1, width), const),
            pl.BlockSpec((1, width), const),
            pl.BlockSpec((LANES, width), const),
        ],
        out_specs=pl.BlockSpec((tb, width), lambda b, c: (tok(b, c), 0)),
        out_shape=jax.ShapeDtypeStruct((t, width), BF16),
        scratch_shapes=[
            pltpu.VMEM((CONV_TAIL + l, conv_dim), BF16),
            pltpu.VMEM((l, conv_dim), F32),
            pltpu.VMEM((SSD_GROUPS, SSD_STATE, width // SSD_GROUPS), F32),
            pltpu.VMEM((2 * l, width), F32),
        ],
        compiler_params=pltpu.CompilerParams(
            dimension_semantics=("arbitrary", "arbitrary"), vmem_limit_bytes=VMEM_LIMIT),
        name="ssd",
    )(proj, proj, dt_t, dtb_col, alog_col, shift, conv_w, conv_b, dskip_e, norm_w, e3)


def _merge_kernel(ya_ref, yb_ref, gl_ref, x_ref, bg_ref, wb_ref, wo_ref, fnw_ref, o_ref, *, final):
    d = x_ref.shape[1]
    bd0 = jnp.dot(ya_ref[...], wb_ref[0], preferred_element_type=F32)
    bd1 = jnp.dot(yb_ref[...], wb_ref[1], preferred_element_type=F32)
    g0 = _sigmoid(gl_ref[:, 0:d].astype(F32) + bg_ref[0:1, :])
    g1 = _sigmoid(gl_ref[:, d:2 * d].astype(F32) + bg_ref[1:2, :])
    merged = (g0 * bd0 + g1 * bd1).astype(BF16)
    xn = x_ref[...] + jnp.dot(merged, wo_ref[...], preferred_element_type=F32)
    if final:
        ms = jnp.mean(xn * xn, axis=-1, keepdims=True)
        xn = xn * lax.rsqrt(ms + EPS) * fnw_ref[...]
    o_ref[...] = xn


def _merge(y_a, y_b, proj, x2d, b_gate, w_branch, w_out, final_norm_w, final):
    t, d = x2d.shape
    e = y_a.shape[1]
    tm = MERGE_TM
    gate_blk = (proj.shape[1] - N_BRANCH * d) // (N_BRANCH * d)
    assert t % tm == 0 and gate_blk * N_BRANCH * d == proj.shape[1] - N_BRANCH * d
    resident = dict(pipeline_mode=pl.Buffered(1))
    return pl.pallas_call(
        functools.partial(_merge_kernel, final=final),
        grid=(t // tm,),
        in_specs=[
            pl.BlockSpec((tm, e), lambda i: (i, 0)),
            pl.BlockSpec((tm, e), lambda i: (i, 0)),
            pl.BlockSpec((tm, N_BRANCH * d), lambda i: (i, gate_blk)),
            pl.BlockSpec((tm, d), lambda i: (i, 0)),
            pl.BlockSpec((N_BRANCH, d), lambda i: (0, 0)),
            pl.BlockSpec((N_BRANCH, e, d), lambda i: (0, 0, 0), **resident),
            pl.BlockSpec((d, d), lambda i: (0, 0), **resident),
            pl.BlockSpec((1, d), lambda i: (0, 0)),
        ],
        out_specs=pl.BlockSpec((tm, d), lambda i: (i, 0)),
        out_shape=jax.ShapeDtypeStruct((t, d), F32),
        compiler_params=pltpu.CompilerParams(
            dimension_semantics=("arbitrary",), vmem_limit_bytes=VMEM_LIMIT),
        name="merge",
    )(y_a, y_b, proj, x2d, b_gate, w_branch, w_out, final_norm_w)


def kernel(x, norm_w, w_in, b_gate, ln_v_w, ln_v_b, w_spatial, b_spatial, conv_w, conv_b,
           dt_bias, a_log, d_skip, ssm_norm_w, w_branch, w_out, final_norm_w):
    batch, seq, d = x.shape
    depth = norm_w.shape[0]
    width = ln_v_w.shape[1]
    n_heads = dt_bias.shape[1]
    conv_k, conv_dim = conv_w.shape[1:]
    assert width == d and n_heads * SSD_HEAD_DIM == width
    o_dt = 4 * width + conv_dim
    n_proj = o_dt + N_BRANCH * d
    t = batch * seq
    x2d = x.reshape(t, d)

    head_of_col = jnp.arange(width, dtype=jnp.int32) // SSD_HEAD_DIM
    e_rows = jnp.arange(LANES, dtype=jnp.int32)
    e3 = ((e_rows[:, None] % n_heads == head_of_col[None, :])
          & (e_rows[:, None] < SPLIT_TERMS * n_heads)).astype(BF16)
    tap = jnp.arange((conv_k - 1) * SSD_CHUNK, dtype=jnp.int32)
    src = CONV_TAIL + tap % SSD_CHUNK - (conv_k - 1) + tap // SSD_CHUNK
    shift = (src[:, None]
             == jnp.arange(CONV_TAIL + SSD_CHUNK, dtype=jnp.int32)[None, :]).astype(BF16)

    for layer in range(depth):
        w_t = _cast_bf16(jnp.swapaxes(w_in[layer], 0, 1))
        proj, dt_t = _inproj(x2d, norm_w[layer][None, :], w_t, o_dt, n_heads)

        y_a = _gmlp(proj, ln_v_w[layer][None, :], ln_v_b[layer][None, :],
                    w_spatial[layer], b_spatial[layer].T, width)

        y_b = _ssd(proj, dt_t, dt_bias[layer][:, None], a_log[layer][:, None], shift,
                   conv_w[layer], conv_b[layer][None, :],
                   jnp.repeat(d_skip[layer], SSD_HEAD_DIM)[None, :], ssm_norm_w[layer][None, :],
                   e3, batch, seq, width)

        x2d = _merge(y_a, y_b, proj, x2d, b_gate[layer], w_branch[layer].astype(BF16),
                     w_out[layer].astype(BF16), final_norm_w[None, :], final=(layer == depth - 1))
    return x2d.reshape(batch, seq, d)
```

```python
import functools

import jax
import jax.numpy as jnp
from jax import lax
from jax.experimental import pallas as pl
from jax.experimental.pallas import tpu as pltpu

F32 = jnp.float32
BF16 = jnp.bfloat16

EPS = 1e-5
GMLP_GROUPS = 8
GMLP_CHUNK = 128
SSD_HEAD_DIM = 64
SSD_GROUPS = 8
SSD_STATE = 128
SSD_CHUNK = 128
N_BRANCH = 2

LANES = 128
SUBLANES = 8
SPLIT_TERMS = 3
CONV_TAIL = 2 * SUBLANES

INPROJ_HEAD_TN = 1024
INPROJ_TM = 1024
INPROJ_TN = 2048
GMLP_TB = 1024
SSD_TB = 512
MERGE_TM = 256
VMEM_LIMIT = 60000 * 1024

NT_DIMS = (((1,), (1,)), ((), ()))
TN_DIMS = (((0,), (0,)), ((), ()))


def _silu(x):
    hx = 0.5 * x
    return hx + hx * jnp.tanh(hx)


def _sigmoid(x):
    return 1.0 / (1.0 + jnp.exp(-x))


def _softplus(x):
    return jnp.maximum(x, 0.0) + jnp.log1p(jnp.exp(-jnp.abs(x)))


def _split_bf16(v, n_terms):
    terms = []
    r = v
    for _ in range(n_terms):
        t = r.astype(BF16).astype(F32)
        terms.append(t)
        r = r - t
    return terms


def _rmsnorm_rows(x_ref, nw_ref, h_ref, row_chunk):
    def body(r, carry):
        r0 = pl.multiple_of(r * row_chunk, row_chunk)
        x = x_ref[pl.ds(r0, row_chunk), :]
        ms = jnp.mean(x * x, axis=-1, keepdims=True)
        h = x * lax.rsqrt(ms + EPS) * nw_ref[...]
        h_ref[pl.ds(r0, row_chunk), :] = h.astype(BF16)
        return carry
    lax.fori_loop(0, x_ref.shape[0] // row_chunk, body, 0)


def _inproj_head_kernel(x_ref, nw_ref, w_ref, wdt_ref, proj_ref, dtt_ref, wbf_ref, wdtbf_ref,
                        h_ref, *, row_chunk):
    @pl.when(pl.program_id(0) == 0)
    def _norm():
        _rmsnorm_rows(x_ref, nw_ref, h_ref, row_chunk)
        wdt = wdt_ref[...].astype(BF16)
        wdtbf_ref[...] = wdt
        dtt_ref[...] = lax.dot_general(wdt, h_ref[...], NT_DIMS, preferred_element_type=F32)

    w = w_ref[...].astype(BF16)
    wbf_ref[...] = w
    proj_ref[...] = lax.dot_general(h_ref[...], w, NT_DIMS,
                                    preferred_element_type=F32).astype(proj_ref.dtype)


def _inproj_kernel(x_ref, nw_ref, w_ref, wdt_ref, proj_in_ref, dtt_in_ref, proj_ref, dtt_ref,
                   h_ref, *, row_chunk):
    del proj_in_ref, dtt_in_ref

    @pl.when(pl.program_id(1) == 0)
    def _norm():
        _rmsnorm_rows(x_ref, nw_ref, h_ref, row_chunk)
        dtt_ref[...] = lax.dot_general(wdt_ref[...], h_ref[...], NT_DIMS,
                                       preferred_element_type=F32)

    proj_ref[...] = lax.dot_general(h_ref[...], w_ref[...], NT_DIMS,
                                    preferred_element_type=F32).astype(proj_ref.dtype)


def _inproj(x2d, norm_w, w_t, o_dt, n_heads):
    t, d = x2d.shape
    n = w_t.shape[0] - n_heads
    tm, tn, tn_h = INPROJ_TM, INPROJ_TN, INPROJ_HEAD_TN
    assert t % tm == 0 and n % tn == 0 and n % tn_h == 0
    assert o_dt % tn_h == 0 and o_dt % n_heads == 0
    out_shape = [jax.ShapeDtypeStruct((t, n), BF16), jax.ShapeDtypeStruct((n_heads, t), F32)]
    w_row = lambda j: (
        pl.multiple_of(j * tn_h + jnp.where(j * tn_h >= o_dt, n_heads, 0), n_heads), 0)
    proj, dt_t, w_bf, w_dt_bf = pl.pallas_call(
        functools.partial(_inproj_head_kernel, row_chunk=128),
        grid=(n // tn_h,),
        in_specs=[
            pl.BlockSpec((tm, d), lambda j: (0, 0)),
            pl.BlockSpec((1, d), lambda j: (0, 0)),
            pl.BlockSpec((pl.Element(tn_h), pl.Element(d)), w_row),
            pl.BlockSpec((n_heads, d), lambda j: (o_dt // n_heads, 0)),
        ],
        out_specs=[
            pl.BlockSpec((tm, tn_h), lambda j: (0, j)),
            pl.BlockSpec((n_heads, tm), lambda j: (0, 0)),
            pl.BlockSpec((tn_h, d), lambda j: (j, 0)),
            pl.BlockSpec((n_heads, d), lambda j: (0, 0)),
        ],
        out_shape=out_shape + [jax.ShapeDtypeStruct((n, d), BF16),
                               jax.ShapeDtypeStruct((n_heads, d), BF16)],
        scratch_shapes=[pltpu.VMEM((tm, d), BF16)],
        compiler_params=pltpu.CompilerParams(
            dimension_semantics=("arbitrary",), vmem_limit_bytes=VMEM_LIMIT),
        name="inproj_head",
    )(x2d, norm_w, w_t, w_t)

    any_spec = pl.BlockSpec(memory_space=pl.ANY)
    return pl.pallas_call(
        functools.partial(_inproj_kernel, row_chunk=128),
        grid=(t // tm - 1, n // tn),
        in_specs=[
            pl.BlockSpec((tm, d), lambda i, j: (i + 1, 0)),
            pl.BlockSpec((1, d), lambda i, j: (0, 0)),
            pl.BlockSpec((tn, d), lambda i, j: (j, 0)),
            pl.BlockSpec((n_heads, d), lambda i, j: (0, 0)),
            any_spec, any_spec,
        ],
        out_specs=[
            pl.BlockSpec((tm, tn), lambda i, j: (i + 1, j)),
            pl.BlockSpec((n_heads, tm), lambda i, j: (0, i + 1)),
        ],
        out_shape=out_shape,
        input_output_aliases={4: 0, 5: 1},
        scratch_shapes=[pltpu.VMEM((tm, d), BF16)],
        compiler_params=pltpu.CompilerParams(
            dimension_semantics=("arbitrary", "arbitrary"),
            vmem_limit_bytes=VMEM_LIMIT),
        name="inproj",
    )(x2d, norm_w, w_bf, w_dt_bf, proj, dt_t)


def _gmlp_kernel(u_ref, v_ref, z_ref, lnw_ref, lnb_ref, ws_ref, bst_ref, y_ref):
    tb, e = u_ref.shape
    l = GMLP_CHUNK
    gd = e // GMLP_GROUPS
    row = lax.broadcasted_iota(jnp.int32, (l, l), 0)
    col = lax.broadcasted_iota(jnp.int32, (l, l), 1)
    causal = col <= row
    wm = [jnp.where(causal, ws_ref[g], 0.0).astype(BF16) for g in range(GMLP_GROUPS)]
    lnw = lnw_ref[...]
    lnb = lnb_ref[...]

    def chunk(ci, carry):
        r0 = pl.multiple_of(ci * l, l)
        v = v_ref[pl.ds(r0, l), :].astype(F32)
        mu = jnp.mean(v, axis=-1, keepdims=True)
        dv = v - mu
        var = jnp.mean(dv * dv, axis=-1, keepdims=True)
        vn = (dv * lax.rsqrt(var + EPS) * lnw + lnb).astype(BF16)
        for g in range(GMLP_GROUPS):
            cs = slice(g * gd, (g + 1) * gd)
            mixed = jnp.dot(wm[g], vn[:, cs], preferred_element_type=F32) + bst_ref[:, g:g + 1]
            gate = _silu(z_ref[pl.ds(r0, l), cs]) * u_ref[pl.ds(r0, l), cs]
            y_ref[pl.ds(r0, l), cs] = gate * mixed.astype(BF16)
        return carry

    lax.fori_loop(0, tb // l, chunk, 0)


def _gmlp(proj, ln_w, ln_b, w_s, b_s_t, width):
    t = proj.shape[0]
    tb = GMLP_TB
    g, l, _ = w_s.shape
    assert l == GMLP_CHUNK and g == GMLP_GROUPS and t % tb == 0
    return pl.pallas_call(
        _gmlp_kernel,
        grid=(t // tb,),
        in_specs=[
            pl.BlockSpec((tb, width), lambda i: (i, 0)),
            pl.BlockSpec((tb, width), lambda i: (i, 1)),
            pl.BlockSpec((tb, width), lambda i: (i, 2)),
            pl.BlockSpec((1, width), lambda i: (0, 0)),
            pl.BlockSpec((1, width), lambda i: (0, 0)),
            pl.BlockSpec((g, l, l), lambda i: (0, 0, 0)),
            pl.BlockSpec((l, g), lambda i: (0, 0)),
        ],
        out_specs=pl.BlockSpec((tb, width), lambda i: (i, 0)),
        out_shape=jax.ShapeDtypeStruct((t, width), BF16),
        compiler_params=pltpu.CompilerParams(
            dimension_semantics=("arbitrary",), vmem_limit_bytes=VMEM_LIMIT),
        name="gmlp",
    )(proj, proj, proj, ln_w, ln_b, w_s, b_s_t)


def _ssd_kernel(zb_ref, xbc_ref, dtt_ref, dtb_ref, alog_ref, shift_ref, cw_ref,
                cbias_ref, dskip_ref, nw_ref, e3_ref, y_ref, xx_ref, xc_ref, ht_ref, we_ref):
    c = pl.program_id(1)
    l = SSD_CHUNK
    n_heads = dtt_ref.shape[0]
    width = zb_ref.shape[1]
    conv_k = cw_ref.shape[0]
    n = SSD_STATE
    g_n = SSD_GROUPS
    gw = width // g_n
    hpg = gw // SSD_HEAD_DIM
    tail = xx_ref.shape[0] - l

    @pl.when(c == 0)
    def _init():
        xx_ref[0:tail, :] = jnp.zeros((tail, xx_ref.shape[1]), BF16)
        ht_ref[...] = jnp.zeros(ht_ref.shape, F32)

    def chunk(ci, carry):
        rows = pl.ds(pl.multiple_of(ci * l, l), l)
        cur = xbc_ref[rows, :]
        xx_ref[tail:tail + l, :] = cur
        shifted = jnp.dot(shift_ref[...], xx_ref[...], preferred_element_type=F32)
        acc = cbias_ref[...]
        for k in range(conv_k - 1):
            acc = acc + shifted[k * l:(k + 1) * l, :] * cw_ref[k:k + 1, :]
        acc = acc + cur.astype(F32) * cw_ref[conv_k - 1:conv_k, :]
        xc_ref[...] = _silu(acc)
        xx_ref[0:tail, :] = cur[l - tail:l, :]

        row = lax.broadcasted_iota(jnp.int32, (l, l), 0)
        col = lax.broadcasted_iota(jnp.int32, (l, l), 1)
        causal = col <= row
        triu = (row <= col).astype(BF16)
        dt_t = _softplus(dtt_ref[:, rows] + dtb_ref[...])
        adt_t = dt_t * (-jnp.exp(alog_ref[...]))
        a_row = None
        for term in _split_bf16(adt_t, SPLIT_TERMS):
            part = jnp.dot(term.astype(BF16), triu, preferred_element_type=F32)
            a_row = part if a_row is None else a_row + part
        a_last = a_row[:, l - 1:l]
        ap = a_row - jnp.log(dt_t)

        zpad = jnp.zeros((LANES - n_heads, l), F32)
        to_col = lambda v: jnp.concatenate([v, zpad], axis=0).T
        a_col = to_col(a_row)

        def pack(v):
            hi, mid, lo = _split_bf16(v, SPLIT_TERMS)
            return (hi + pltpu.roll(mid, n_heads, axis=1)
                    + pltpu.roll(lo, 2 * n_heads, axis=1)).astype(BF16)

        e3 = e3_ref[...]
        we_ref[0:l, :] = jnp.dot(pack(to_col(dt_t * jnp.exp(a_last - a_row))), e3,
                                 preferred_element_type=F32)
        we_ref[l:2 * l, :] = jnp.dot(pack(to_col(jnp.exp(a_row))), e3, preferred_element_type=F32)

        lane_g = lax.broadcasted_iota(jnp.int32, (l, gw), 1)
        for g in range(g_n):
            gs = slice(g * gw, (g + 1) * gw)
            b_g = xc_ref[:, width + g * n: width + (g + 1) * n].astype(BF16)
            c_g = xc_ref[:, width + (g_n + g) * n: width + (g_n + g + 1) * n].astype(BF16)
            cb = lax.dot_general(c_g, b_g, NT_DIMS, preferred_element_type=F32)
            cb = jnp.where(causal, cb, 0.0)
            xs_g = xc_ref[:, gs]
            xs_bf = xs_g.astype(BF16)

            scores = []
            xbd = []
            for k in range(hpg):
                h = g * hpg + k
                seg = a_col[:, h:h + 1] - ap[h:h + 1, :]
                scores.append((cb * jnp.exp(jnp.where(causal, seg, 0.0))).astype(BF16))
                in_head = (lane_g >= k * SSD_HEAD_DIM) & (lane_g < (k + 1) * SSD_HEAD_DIM)
                xbd.append(jnp.where(in_head, xs_bf, jnp.zeros_like(xs_bf)))
            s_cat = jnp.concatenate(scores, axis=1)
            x_bd = jnp.concatenate(xbd, axis=0)
            y_diag = jnp.dot(s_cat, x_bd, preferred_element_type=F32)

            h_prev = ht_ref[g]
            y_off = jnp.dot(c_g, h_prev.astype(BF16), preferred_element_type=F32) * we_ref[l:2 * l, gs]
            x_dec = (xs_g * we_ref[0:l, gs]).astype(BF16)
            st = lax.dot_general(b_g, x_dec, TN_DIMS, preferred_element_type=F32)
            ht_ref[g] = h_prev * we_ref[2 * l - 1:2 * l, gs] + st

            y = y_diag + y_off + dskip_ref[:, gs] * xs_g
            y = y * _silu(zb_ref[rows, gs]).astype(F32)
            ms = jnp.mean(y * y, axis=-1, keepdims=True)
            y_ref[rows, gs] = (y * lax.rsqrt(ms + EPS) * nw_ref[:, gs]).astype(y_ref.dtype)
        return carry

    lax.fori_loop(0, zb_ref.shape[0] // l, chunk, 0)


def _ssd(proj, dt_t, dtb_col, alog_col, shift, conv_w, conv_b, dskip_e, norm_w, e3, batch, seq,
         width):
    t = proj.shape[0]
    n_heads = dt_t.shape[0]
    l = SSD_CHUNK
    conv_k, conv_dim = conv_w.shape
    assert conv_dim == width + 2 * SSD_GROUPS * SSD_STATE and (4 * width) % conv_dim == 0
    assert width % SSD_GROUPS == 0 and conv_k - 1 <= l and SPLIT_TERMS * n_heads <= LANES
    assert shift.shape == ((conv_k - 1) * l, CONV_TAIL + l) and conv_k - 1 <= CONV_TAIL
    zb_blk = 3
    xbc_blk = (4 * width) // conv_dim
    tb = SSD_TB
    nb = seq // tb
    assert seq % tb == 0 and tb % l == 0
    tok = lambda b, c: b * nb + c
    const = lambda b, c: (0, 0)
    return pl.pallas_call(
        _ssd_kernel,
        grid=(batch, nb),
        in_specs=[
            pl.BlockSpec((tb, width), lambda b, c: (tok(b, c), zb_blk)),
            pl.BlockSpec((tb, conv_dim), lambda b, c: (tok(b, c), xbc_blk)),
            pl.BlockSpec((n_heads, tb), lambda b, c: (0, tok(b, c))),
            pl.BlockSpec((n_heads, 1), const),
            pl.BlockSpec((n_heads, 1), const),
            pl.BlockSpec(shift.shape, const),
            pl.BlockSpec((conv_k, conv_dim), const),
            pl.BlockSpec((1, conv_dim), const),
            pl.BlockSpec((1, width), const),
            pl.BlockSpec((1, width), const),
            pl.BlockSpec((LANES, width), const),
        ],
        out_specs=pl.BlockSpec((tb, width), lambda b, c: (tok(b, c), 0)),
        out_shape=jax.ShapeDtypeStruct((t, width), BF16),
        scratch_shapes=[
            pltpu.VMEM((CONV_TAIL + l, conv_dim), BF16),
            pltpu.VMEM((l, conv_dim), F32),
            pltpu.VMEM((SSD_GROUPS, SSD_STATE, width // SSD_GROUPS), F32),
            pltpu.VMEM((2 * l, width), F32),
        ],
        compiler_params=pltpu.CompilerParams(
            dimension_semantics=("arbitrary", "arbitrary"), vmem_limit_bytes=VMEM_LIMIT),
        name="ssd",
    )(proj, proj, dt_t, dtb_col, alog_col, shift, conv_w, conv_b, dskip_e, norm_w, e3)


def _merge_kernel(ya_ref, yb_ref, gl_ref, x_ref, bg_ref, wb_ref, wo_ref, fnw_ref, o_ref, *, final):
    d = x_ref.shape[1]
    bd0 = jnp.dot(ya_ref[...], wb_ref[0], preferred_element_type=F32)
    bd1 = jnp.dot(yb_ref[...], wb_ref[1], preferred_element_type=F32)
    g0 = _sigmoid(gl_ref[:, 0:d].astype(F32) + bg_ref[0:1, :])
    g1 = _sigmoid(gl_ref[:, d:2 * d].astype(F32) + bg_ref[1:2, :])
    merged = (g0 * bd0 + g1 * bd1).astype(BF16)
    xn = x_ref[...] + jnp.dot(merged, wo_ref[...], preferred_element_type=F32)
    if final:
        ms = jnp.mean(xn * xn, axis=-1, keepdims=True)
        xn = xn * lax.rsqrt(ms + EPS) * fnw_ref[...]
    o_ref[...] = xn


def _merge(y_a, y_b, proj, x2d, b_gate, w_branch, w_out, final_norm_w, final):
    t, d = x2d.shape
    e = y_a.shape[1]
    tm = MERGE_TM
    gate_blk = (proj.shape[1] - N_BRANCH * d) // (N_BRANCH * d)
    assert t % tm == 0 and gate_blk * N_BRANCH * d == proj.shape[1] - N_BRANCH * d
    resident = dict(pipeline_mode=pl.Buffered(1))
    return pl.pallas_call(
        functools.partial(_merge_kernel, final=final),
        grid=(t // tm,),
        in_specs=[
            pl.BlockSpec((tm, e), lambda i: (i, 0)),
            pl.BlockSpec((tm, e), lambda i: (i, 0)),
            pl.BlockSpec((tm, N_BRANCH * d), lambda i: (i, gate_blk)),
            pl.BlockSpec((tm, d), lambda i: (i, 0)),
            pl.BlockSpec((N_BRANCH, d), lambda i: (0, 0)),
            pl.BlockSpec((N_BRANCH, e, d), lambda i: (0, 0, 0), **resident),
            pl.BlockSpec((d, d), lambda i: (0, 0), **resident),
            pl.BlockSpec((1, d), lambda i: (0, 0)),
        ],
        out_specs=pl.BlockSpec((tm, d), lambda i: (i, 0)),
        out_shape=jax.ShapeDtypeStruct((t, d), F32),
        compiler_params=pltpu.CompilerParams(
            dimension_semantics=("arbitrary",), vmem_limit_bytes=VMEM_LIMIT),
        name="merge",
    )(y_a, y_b, proj, x2d, b_gate, w_branch, w_out, final_norm_w)


def kernel(x, norm_w, w_in, b_gate, ln_v_w, ln_v_b, w_spatial, b_spatial, conv_w, conv_b,
           dt_bias, a_log, d_skip, ssm_norm_w, w_branch, w_out, final_norm_w):
    batch, seq, d = x.shape
    depth = norm_w.shape[0]
    width = ln_v_w.shape[1]
    n_heads = dt_bias.shape[1]
    conv_k, conv_dim = conv_w.shape[1:]
    assert width == d and n_heads * SSD_HEAD_DIM == width
    o_dt = 4 * width + conv_dim
    n_proj = o_dt + N_BRANCH * d
    t = batch * seq
    x2d = x.reshape(t, d)

    head_of_col = jnp.arange(width, dtype=jnp.int32) // SSD_HEAD_DIM
    e_rows = jnp.arange(LANES, dtype=jnp.int32)
    e3 = ((e_rows[:, None] % n_heads == head_of_col[None, :])
          & (e_rows[:, None] < SPLIT_TERMS * n_heads)).astype(BF16)
    tap = jnp.arange((conv_k - 1) * SSD_CHUNK, dtype=jnp.int32)
    src = CONV_TAIL + tap % SSD_CHUNK - (conv_k - 1) + tap // SSD_CHUNK
    shift = (src[:, None]
             == jnp.arange(CONV_TAIL + SSD_CHUNK, dtype=jnp.int32)[None, :]).astype(BF16)

    for layer in range(depth):
        proj, dt_t = _inproj(x2d, norm_w[layer][None, :], jnp.swapaxes(w_in[layer], 0, 1), o_dt,
                             n_heads)

        y_a = _gmlp(proj, ln_v_w[layer][None, :], ln_v_b[layer][None, :],
                    w_spatial[layer], b_spatial[layer].T, width)

        y_b = _ssd(proj, dt_t, dt_bias[layer][:, None], a_log[layer][:, None], shift,
                   conv_w[layer], conv_b[layer][None, :],
                   jnp.repeat(d_skip[layer], SSD_HEAD_DIM)[None, :], ssm_norm_w[layer][None, :],
                   e3, batch, seq, width)

        x2d = _merge(y_a, y_b, proj, x2d, b_gate[layer], w_branch[layer].astype(BF16),
                     w_out[layer].astype(BF16), final_norm_w[None, :], final=(layer == depth - 1))
    return x2d.reshape(batch, seq, d)
```

```python
import functools

import jax
import jax.numpy as jnp
from jax import lax
from jax.experimental import pallas as pl
from jax.experimental.pallas import tpu as pltpu

F32 = jnp.float32
BF16 = jnp.bfloat16

EPS = 1e-5
GMLP_GROUPS = 8
GMLP_CHUNK = 128
SSD_HEAD_DIM = 64
SSD_GROUPS = 8
SSD_STATE = 128
SSD_CHUNK = 128
N_BRANCH = 2

LANES = 128
SUBLANES = 8
SPLIT_TERMS = 3
CONV_TAIL = 2 * SUBLANES

INPROJ_HEAD_TN = 1024
INPROJ_TM = 1024
INPROJ_TN = 2048
SSD_TB = 512
MERGE_TM = 256
VMEM_LIMIT = 60000 * 1024

NT_DIMS = (((1,), (1,)), ((), ()))
TN_DIMS = (((0,), (0,)), ((), ()))


def _silu(x):
    hx = 0.5 * x
    return hx + hx * jnp.tanh(hx)


def _sigmoid(x):
    return 1.0 / (1.0 + jnp.exp(-x))


def _softplus(x):
    return jnp.maximum(x, 0.0) + jnp.log1p(jnp.exp(-jnp.abs(x)))


def _split_bf16(v, n_terms):
    terms = []
    r = v
    for _ in range(n_terms):
        t = r.astype(BF16).astype(F32)
        terms.append(t)
        r = r - t
    return terms


def _rmsnorm_rows(x_ref, nw_ref, h_ref, row_chunk):
    def body(r, carry):
        r0 = pl.multiple_of(r * row_chunk, row_chunk)
        x = x_ref[pl.ds(r0, row_chunk), :]
        ms = jnp.mean(x * x, axis=-1, keepdims=True)
        h = x * lax.rsqrt(ms + EPS) * nw_ref[...]
        h_ref[pl.ds(r0, row_chunk), :] = h.astype(BF16)
        return carry
    lax.fori_loop(0, x_ref.shape[0] // row_chunk, body, 0)


def _inproj_head_kernel(x_ref, nw_ref, w_ref, wdt_ref, proj_ref, dtt_ref, wbf_ref, wdtbf_ref,
                        h_ref, *, row_chunk):
    @pl.when(pl.program_id(0) == 0)
    def _norm():
        _rmsnorm_rows(x_ref, nw_ref, h_ref, row_chunk)
        wdt = wdt_ref[...].astype(BF16)
        wdtbf_ref[...] = wdt
        dtt_ref[...] = lax.dot_general(wdt, h_ref[...], NT_DIMS, preferred_element_type=F32)

    w = w_ref[...].astype(BF16)
    wbf_ref[...] = w
    proj_ref[...] = lax.dot_general(h_ref[...], w, NT_DIMS,
                                    preferred_element_type=F32).astype(proj_ref.dtype)


def _inproj_kernel(x_ref, nw_ref, w_ref, wdt_ref, proj_in_ref, dtt_in_ref, proj_ref, dtt_ref,
                   h_ref, *, row_chunk):
    del proj_in_ref, dtt_in_ref

    @pl.when(pl.program_id(1) == 0)
    def _norm():
        _rmsnorm_rows(x_ref, nw_ref, h_ref, row_chunk)
        dtt_ref[...] = lax.dot_general(wdt_ref[...], h_ref[...], NT_DIMS,
                                       preferred_element_type=F32)

    proj_ref[...] = lax.dot_general(h_ref[...], w_ref[...], NT_DIMS,
                                    preferred_element_type=F32).astype(proj_ref.dtype)


def _inproj(x2d, norm_w, w_t, o_dt, n_heads):
    t, d = x2d.shape
    n = w_t.shape[0] - n_heads
    tm, tn, tn_h = INPROJ_TM, INPROJ_TN, INPROJ_HEAD_TN
    assert t % tm == 0 and n % tn == 0 and n % tn_h == 0
    assert o_dt % tn_h == 0 and o_dt % n_heads == 0
    out_shape = [jax.ShapeDtypeStruct((t, n), BF16), jax.ShapeDtypeStruct((n_heads, t), F32)]
    w_row = lambda j: (
        pl.multiple_of(j * tn_h + jnp.where(j * tn_h >= o_dt, n_heads, 0), n_heads), 0)
    proj, dt_t, w_bf, w_dt_bf = pl.pallas_call(
        functools.partial(_inproj_head_kernel, row_chunk=128),
        grid=(n // tn_h,),
        in_specs=[
            pl.BlockSpec((tm, d), lambda j: (0, 0)),
            pl.BlockSpec((1, d), lambda j: (0, 0)),
            pl.BlockSpec((pl.Element(tn_h), pl.Element(d)), w_row),
            pl.BlockSpec((n_heads, d), lambda j: (o_dt // n_heads, 0)),
        ],
        out_specs=[
            pl.BlockSpec((tm, tn_h), lambda j: (0, j)),
            pl.BlockSpec((n_heads, tm), lambda j: (0, 0)),
            pl.BlockSpec((tn_h, d), lambda j: (j, 0)),
            pl.BlockSpec((n_heads, d), lambda j: (0, 0)),
        ],
        out_shape=out_shape + [jax.ShapeDtypeStruct((n, d), BF16),
                               jax.ShapeDtypeStruct((n_heads, d), BF16)],
        scratch_shapes=[pltpu.VMEM((tm, d), BF16)],
        compiler_params=pltpu.CompilerParams(
            dimension_semantics=("arbitrary",), vmem_limit_bytes=VMEM_LIMIT),
        name="inproj_head",
    )(x2d, norm_w, w_t, w_t)

    any_spec = pl.BlockSpec(memory_space=pl.ANY)
    return pl.pallas_call(
        functools.partial(_inproj_kernel, row_chunk=128),
        grid=(t // tm - 1, n // tn),
        in_specs=[
            pl.BlockSpec((tm, d), lambda i, j: (i + 1, 0)),
            pl.BlockSpec((1, d), lambda i, j: (0, 0)),
            pl.BlockSpec((tn, d), lambda i, j: (j, 0)),
            pl.BlockSpec((n_heads, d), lambda i, j: (0, 0)),
            any_spec, any_spec,
        ],
        out_specs=[
            pl.BlockSpec((tm, tn), lambda i, j: (i + 1, j)),
            pl.BlockSpec((n_heads, tm), lambda i, j: (0, i + 1)),
        ],
        out_shape=out_shape,
        input_output_aliases={4: 0, 5: 1},
        scratch_shapes=[pltpu.VMEM((tm, d), BF16)],
        compiler_params=pltpu.CompilerParams(
            dimension_semantics=("arbitrary", "arbitrary"),
            vmem_limit_bytes=VMEM_LIMIT),
        name="inproj",
    )(x2d, norm_w, w_bf, w_dt_bf, proj, dt_t)


def _gmlp_rows(u_ref, v_ref, z_ref, lnw_ref, lnb_ref, ws_ref, bst_ref, y_ref):
    tb, e = u_ref.shape
    l = GMLP_CHUNK
    gd = e // GMLP_GROUPS
    row = lax.broadcasted_iota(jnp.int32, (l, l), 0)
    col = lax.broadcasted_iota(jnp.int32, (l, l), 1)
    causal = col <= row
    wm = [jnp.where(causal, ws_ref[g], 0.0).astype(BF16) for g in range(GMLP_GROUPS)]
    lnw = lnw_ref[...]
    lnb = lnb_ref[...]

    for ci in range(tb // l):
        rows = slice(ci * l, (ci + 1) * l)
        v = v_ref[rows, :].astype(F32)
        mu = jnp.mean(v, axis=-1, keepdims=True)
        dv = v - mu
        var = jnp.mean(dv * dv, axis=-1, keepdims=True)
        vn = (dv * lax.rsqrt(var + EPS) * lnw + lnb).astype(BF16)
        for g in range(GMLP_GROUPS):
            cs = slice(g * gd, (g + 1) * gd)
            mixed = jnp.dot(wm[g], vn[:, cs], preferred_element_type=F32) + bst_ref[:, g:g + 1]
            gate = _silu(z_ref[rows, cs]) * u_ref[rows, cs]
            y_ref[rows, cs] = gate * mixed.astype(BF16)


def _ssd_kernel(zb_ref, xbc_ref, dtt_ref, dtb_ref, alog_ref, shift_ref, cw_ref,
                cbias_ref, dskip_ref, nw_ref, e3_ref, y_ref, xx_ref, xc_ref, ht_ref, we_ref):
    c = pl.program_id(1)
    l = SSD_CHUNK
    n_heads = dtt_ref.shape[0]
    width = zb_ref.shape[1]
    conv_k = cw_ref.shape[0]
    n = SSD_STATE
    g_n = SSD_GROUPS
    gw = width // g_n
    hpg = gw // SSD_HEAD_DIM
    tail = xx_ref.shape[0] - l

    @pl.when(c == 0)
    def _init():
        xx_ref[0:tail, :] = jnp.zeros((tail, xx_ref.shape[1]), BF16)
        ht_ref[...] = jnp.zeros(ht_ref.shape, F32)

    def chunk(ci, carry):
        rows = pl.ds(pl.multiple_of(ci * l, l), l)
        cur = xbc_ref[rows, :]
        xx_ref[tail:tail + l, :] = cur
        shifted = jnp.dot(shift_ref[...], xx_ref[...], preferred_element_type=F32)
        acc = cbias_ref[...]
        for k in range(conv_k - 1):
            acc = acc + shifted[k * l:(k + 1) * l, :] * cw_ref[k:k + 1, :]
        acc = acc + cur.astype(F32) * cw_ref[conv_k - 1:conv_k, :]
        xc_ref[...] = _silu(acc)
        xx_ref[0:tail, :] = cur[l - tail:l, :]

        row = lax.broadcasted_iota(jnp.int32, (l, l), 0)
        col = lax.broadcasted_iota(jnp.int32, (l, l), 1)
        causal = col <= row
        triu = (row <= col).astype(BF16)
        dt_t = _softplus(dtt_ref[:, rows] + dtb_ref[...])
        adt_t = dt_t * (-jnp.exp(alog_ref[...]))
        a_row = None
        for term in _split_bf16(adt_t, SPLIT_TERMS):
            part = jnp.dot(term.astype(BF16), triu, preferred_element_type=F32)
            a_row = part if a_row is None else a_row + part
        a_last = a_row[:, l - 1:l]
        ap = a_row - jnp.log(dt_t)

        zpad = jnp.zeros((LANES - n_heads, l), F32)
        to_col = lambda v: jnp.concatenate([v, zpad], axis=0).T
        a_col = to_col(a_row)

        def pack(v):
            hi, mid, lo = _split_bf16(v, SPLIT_TERMS)
            return (hi + pltpu.roll(mid, n_heads, axis=1)
                    + pltpu.roll(lo, 2 * n_heads, axis=1)).astype(BF16)

        e3 = e3_ref[...]
        we_ref[0:l, :] = jnp.dot(pack(to_col(dt_t * jnp.exp(a_last - a_row))), e3,
                                 preferred_element_type=F32)
        we_ref[l:2 * l, :] = jnp.dot(pack(to_col(jnp.exp(a_row))), e3, preferred_element_type=F32)

        lane_g = lax.broadcasted_iota(jnp.int32, (l, gw), 1)
        for g in range(g_n):
            gs = slice(g * gw, (g + 1) * gw)
            b_g = xc_ref[:, width + g * n: width + (g + 1) * n].astype(BF16)
            c_g = xc_ref[:, width + (g_n + g) * n: width + (g_n + g + 1) * n].astype(BF16)
            cb = lax.dot_general(c_g, b_g, NT_DIMS, preferred_element_type=F32)
            cb = jnp.where(causal, cb, 0.0)
            xs_g = xc_ref[:, gs]
            xs_bf = xs_g.astype(BF16)

            scores = []
            xbd = []
            for k in range(hpg):
                h = g * hpg + k
                seg = a_col[:, h:h + 1] - ap[h:h + 1, :]
                scores.append((cb * jnp.exp(jnp.where(causal, seg, 0.0))).astype(BF16))
                in_head = (lane_g >= k * SSD_HEAD_DIM) & (lane_g < (k + 1) * SSD_HEAD_DIM)
                xbd.append(jnp.where(in_head, xs_bf, jnp.zeros_like(xs_bf)))
            s_cat = jnp.concatenate(scores, axis=1)
            x_bd = jnp.concatenate(xbd, axis=0)
            y_diag = jnp.dot(s_cat, x_bd, preferred_element_type=F32)

            h_prev = ht_ref[g]
            y_off = jnp.dot(c_g, h_prev.astype(BF16), preferred_element_type=F32) * we_ref[l:2 * l, gs]
            x_dec = (xs_g * we_ref[0:l, gs]).astype(BF16)
            st = lax.dot_general(b_g, x_dec, TN_DIMS, preferred_element_type=F32)
            ht_ref[g] = h_prev * we_ref[2 * l - 1:2 * l, gs] + st

            y = y_diag + y_off + dskip_ref[:, gs] * xs_g
            y = y * _silu(zb_ref[rows, gs]).astype(F32)
            ms = jnp.mean(y * y, axis=-1, keepdims=True)
            y_ref[rows, gs] = (y * lax.rsqrt(ms + EPS) * nw_ref[:, gs]).astype(y_ref.dtype)
        return carry

    lax.fori_loop(0, zb_ref.shape[0] // l, chunk, 0)


def _ssd(proj, dt_t, dtb_col, alog_col, shift, conv_w, conv_b, dskip_e, norm_w, e3, batch, seq,
         width):
    t = proj.shape[0]
    n_heads = dt_t.shape[0]
    l = SSD_CHUNK
    conv_k, conv_dim = conv_w.shape
    assert conv_dim == width + 2 * SSD_GROUPS * SSD_STATE and (4 * width) % conv_dim == 0
    assert width % SSD_GROUPS == 0 and conv_k - 1 <= l and SPLIT_TERMS * n_heads <= LANES
    assert shift.shape == ((conv_k - 1) * l, CONV_TAIL + l) and conv_k - 1 <= CONV_TAIL
    zb_blk = 3
    xbc_blk = (4 * width) // conv_dim
    tb = SSD_TB
    nb = seq // tb
    assert seq % tb == 0 and tb % l == 0
    tok = lambda b, c: b * nb + c
    const = lambda b, c: (0, 0)
    return pl.pallas_call(
        _ssd_kernel,
        grid=(batch, nb),
        in_specs=[
            pl.BlockSpec((tb, width), lambda b, c: (tok(b, c), zb_blk)),
            pl.BlockSpec((tb, conv_dim), lambda b, c: (tok(b, c), xbc_blk)),
            pl.BlockSpec((n_heads, tb), lambda b, c: (0, tok(b, c))),
            pl.BlockSpec((n_heads, 1), const),
            pl.BlockSpec((n_heads, 1), const),
            pl.BlockSpec(shift.shape, const),
            pl.BlockSpec((conv_k, conv_dim), const),
            pl.BlockSpec((1, conv_dim), const),
            pl.BlockSpec((1, width), const),
            pl.BlockSpec((1, width), const),
            pl.BlockSpec((LANES, width), const),
        ],
        out_specs=pl.BlockSpec((tb, width), lambda b, c: (tok(b, c), 0)),
        out_shape=jax.ShapeDtypeStruct((t, width), BF16),
        scratch_shapes=[
            pltpu.VMEM((CONV_TAIL + l, conv_dim), BF16),
            pltpu.VMEM((l, conv_dim), F32),
            pltpu.VMEM((SSD_GROUPS, SSD_STATE, width // SSD_GROUPS), F32),
            pltpu.VMEM((2 * l, width), F32),
        ],
        compiler_params=pltpu.CompilerParams(
            dimension_semantics=("arbitrary", "arbitrary"), vmem_limit_bytes=VMEM_LIMIT),
        name="ssd",
    )(proj, proj, dt_t, dtb_col, alog_col, shift, conv_w, conv_b, dskip_e, norm_w, e3)


def _merge_kernel(u_ref, v_ref, za_ref, lnw_ref, lnb_ref, ws_ref, bst_ref, yb_ref, gl_ref, x_ref,
                  bg_ref, wb_ref, wo_ref, fnw_ref, o_ref, ya_ref, *, final):
    d = x_ref.shape[1]
    bd1 = jnp.dot(yb_ref[...], wb_ref[1], preferred_element_type=F32)
    _gmlp_rows(u_ref, v_ref, za_ref, lnw_ref, lnb_ref, ws_ref, bst_ref, ya_ref)
    bd0 = jnp.dot(ya_ref[...], wb_ref[0], preferred_element_type=F32)
    g0 = _sigmoid(gl_ref[:, 0:d].astype(F32) + bg_ref[0:1, :])
    g1 = _sigmoid(gl_ref[:, d:2 * d].astype(F32) + bg_ref[1:2, :])
    merged = (g0 * bd0 + g1 * bd1).astype(BF16)
    xn = x_ref[...] + jnp.dot(merged, wo_ref[...], preferred_element_type=F32)
    if final:
        ms = jnp.mean(xn * xn, axis=-1, keepdims=True)
        xn = xn * lax.rsqrt(ms + EPS) * fnw_ref[...]
    o_ref[...] = xn


def _merge(proj, ln_w, ln_b, w_s, b_s_t, y_b, x2d, b_gate, w_branch, w_out, final_norm_w, final):
    t, d = x2d.shape
    e = y_b.shape[1]
    tm = MERGE_TM
    g, l, _ = w_s.shape
    gate_blk = (proj.shape[1] - N_BRANCH * d) // (N_BRANCH * d)
    assert t % tm == 0 and gate_blk * N_BRANCH * d == proj.shape[1] - N_BRANCH * d
    assert l == GMLP_CHUNK and g == GMLP_GROUPS and tm % l == 0
    resident = dict(pipeline_mode=pl.Buffered(1))
    return pl.pallas_call(
        functools.partial(_merge_kernel, final=final),
        grid=(t // tm,),
        in_specs=[
            pl.BlockSpec((tm, e), lambda i: (i, 0)),
            pl.BlockSpec((tm, e), lambda i: (i, 1)),
            pl.BlockSpec((tm, e), lambda i: (i, 2)),
            pl.BlockSpec((1, e), lambda i: (0, 0)),
            pl.BlockSpec((1, e), lambda i: (0, 0)),
            pl.BlockSpec((g, l, l), lambda i: (0, 0, 0)),
            pl.BlockSpec((l, g), lambda i: (0, 0)),
            pl.BlockSpec((tm, e), lambda i: (i, 0)),
            pl.BlockSpec((tm, N_BRANCH * d), lambda i: (i, gate_blk)),
            pl.BlockSpec((tm, d), lambda i: (i, 0)),
            pl.BlockSpec((N_BRANCH, d), lambda i: (0, 0)),
            pl.BlockSpec((N_BRANCH, e, d), lambda i: (0, 0, 0), **resident),
            pl.BlockSpec((d, d), lambda i: (0, 0), **resident),
            pl.BlockSpec((1, d), lambda i: (0, 0)),
        ],
        out_specs=pl.BlockSpec((tm, d), lambda i: (i, 0)),
        out_shape=jax.ShapeDtypeStruct((t, d), F32),
        scratch_shapes=[pltpu.VMEM((tm, e), BF16)],
        compiler_params=pltpu.CompilerParams(
            dimension_semantics=("arbitrary",), vmem_limit_bytes=VMEM_LIMIT),
        name="merge",
    )(proj, proj, proj, ln_w, ln_b, w_s, b_s_t, y_b, proj, x2d, b_gate, w_branch, w_out,
      final_norm_w)


def kernel(x, norm_w, w_in, b_gate, ln_v_w, ln_v_b, w_spatial, b_spatial, conv_w, conv_b,
           dt_bias, a_log, d_skip, ssm_norm_w, w_branch, w_out, final_norm_w):
    batch, seq, d = x.shape
    depth = norm_w.shape[0]
    width = ln_v_w.shape[1]
    n_heads = dt_bias.shape[1]
    conv_k, conv_dim = conv_w.shape[1:]
    assert width == d and n_heads * SSD_HEAD_DIM == width
    o_dt = 4 * width + conv_dim
    n_proj = o_dt + N_BRANCH * d
    t = batch * seq
    x2d = x.reshape(t, d)

    head_of_col = jnp.arange(width, dtype=jnp.int32) // SSD_HEAD_DIM
    e_rows = jnp.arange(LANES, dtype=jnp.int32)
    e3 = ((e_rows[:, None] % n_heads == head_of_col[None, :])
          & (e_rows[:, None] < SPLIT_TERMS * n_heads)).astype(BF16)
    tap = jnp.arange((conv_k - 1) * SSD_CHUNK, dtype=jnp.int32)
    src = CONV_TAIL + tap % SSD_CHUNK - (conv_k - 1) + tap // SSD_CHUNK
    shift = (src[:, None]
             == jnp.arange(CONV_TAIL + SSD_CHUNK, dtype=jnp.int32)[None, :]).astype(BF16)

    for layer in range(depth):
        proj, dt_t = _inproj(x2d, norm_w[layer][None, :], jnp.swapaxes(w_in[layer], 0, 1), o_dt,
                             n_heads)

        y_b = _ssd(proj, dt_t, dt_bias[layer][:, None], a_log[layer][:, None], shift,
                   conv_w[layer], conv_b[layer][None, :],
                   jnp.repeat(d_skip[layer], SSD_HEAD_DIM)[None, :], ssm_norm_w[layer][None, :],
                   e3, batch, seq, width)

        x2d = _merge(proj, ln_v_w[layer][None, :], ln_v_b[layer][None, :], w_spatial[layer],
                     b_spatial[layer].T, y_b, x2d, b_gate[layer], w_branch[layer].astype(BF16),
                     w_out[layer].astype(BF16), final_norm_w[None, :], final=(layer == depth - 1))
    return x2d.reshape(batch, seq, d)
```

```python
import functools

import jax
import jax.numpy as jnp
from jax import lax
from jax.experimental import pallas as pl
from jax.experimental.pallas import tpu as pltpu

F32 = jnp.float32
BF16 = jnp.bfloat16

EPS = 1e-5
GMLP_GROUPS = 8
GMLP_CHUNK = 128
SSD_HEAD_DIM = 64
SSD_GROUPS = 8
SSD_STATE = 128
SSD_CHUNK = 128
N_BRANCH = 2

LANES = 128
SUBLANES = 8
SPLIT_TERMS = 3
CONV_TAIL = 2 * SUBLANES

INPROJ_HEAD_TN = 1024
INPROJ_TM = 1024
INPROJ_TN = 2048
SSD_TB = 512
MERGE_TM = 256
VMEM_LIMIT = 60000 * 1024

NT_DIMS = (((1,), (1,)), ((), ()))
TN_DIMS = (((0,), (0,)), ((), ()))


def _silu(x):
    hx = 0.5 * x
    return hx + hx * jnp.tanh(hx)


def _sigmoid(x):
    return 1.0 / (1.0 + jnp.exp(-x))


def _softplus(x):
    return jnp.maximum(x, 0.0) + jnp.log1p(jnp.exp(-jnp.abs(x)))


def _split_bf16(v, n_terms):
    terms = []
    r = v
    for _ in range(n_terms):
        t = r.astype(BF16).astype(F32)
        terms.append(t)
        r = r - t
    return terms


def _rmsnorm_rows(x_ref, nw_ref, h_ref, row_chunk):
    def body(r, carry):
        r0 = pl.multiple_of(r * row_chunk, row_chunk)
        x = x_ref[pl.ds(r0, row_chunk), :]
        ms = jnp.mean(x * x, axis=-1, keepdims=True)
        h = x * lax.rsqrt(ms + EPS) * nw_ref[...]
        h_ref[pl.ds(r0, row_chunk), :] = h.astype(BF16)
        return carry
    lax.fori_loop(0, x_ref.shape[0] // row_chunk, body, 0)


def _inproj_head_kernel(x_ref, nw_ref, w_ref, wdt_ref, proj_ref, dtt_ref, wbf_ref, wdtbf_ref,
                        h_ref, *, row_chunk):
    @pl.when(pl.program_id(0) == 0)
    def _norm():
        _rmsnorm_rows(x_ref, nw_ref, h_ref, row_chunk)
        wdt = wdt_ref[...].astype(BF16)
        wdtbf_ref[...] = wdt
        dtt_ref[...] = lax.dot_general(wdt, h_ref[...], NT_DIMS, preferred_element_type=F32)

    w = w_ref[...].astype(BF16)
    wbf_ref[...] = w
    proj_ref[...] = lax.dot_general(h_ref[...], w, NT_DIMS,
                                    preferred_element_type=F32).astype(proj_ref.dtype)


def _inproj_kernel(x_ref, nw_ref, w_ref, wdt_ref, proj_in_ref, dtt_in_ref, proj_ref, dtt_ref,
                   h_ref, *, row_chunk):
    del proj_in_ref, dtt_in_ref

    @pl.when(pl.program_id(1) == 0)
    def _norm():
        _rmsnorm_rows(x_ref, nw_ref, h_ref, row_chunk)
        dtt_ref[...] = lax.dot_general(wdt_ref[...], h_ref[...], NT_DIMS,
                                       preferred_element_type=F32)

    proj_ref[...] = lax.dot_general(h_ref[...], w_ref[...], NT_DIMS,
                                    preferred_element_type=F32).astype(proj_ref.dtype)


def _inproj(x2d, norm_w, w_t, o_dt, n_heads):
    t, d = x2d.shape
    n = w_t.shape[0] - n_heads
    tm, tn, tn_h = INPROJ_TM, INPROJ_TN, INPROJ_HEAD_TN
    assert t % tm == 0 and n % tn == 0 and n % tn_h == 0
    assert o_dt % tn_h == 0 and o_dt % n_heads == 0
    out_shape = [jax.ShapeDtypeStruct((t, n), BF16), jax.ShapeDtypeStruct((n_heads, t), F32)]
    w_row = lambda j: (
        pl.multiple_of(j * tn_h + jnp.where(j * tn_h >= o_dt, n_heads, 0), n_heads), 0)
    proj, dt_t, w_bf, w_dt_bf = pl.pallas_call(
        functools.partial(_inproj_head_kernel, row_chunk=128),
        grid=(n // tn_h,),
        in_specs=[
            pl.BlockSpec((tm, d), lambda j: (0, 0)),
            pl.BlockSpec((1, d), lambda j: (0, 0)),
            pl.BlockSpec((pl.Element(tn_h), pl.Element(d)), w_row),
            pl.BlockSpec((n_heads, d), lambda j: (o_dt // n_heads, 0)),
        ],
        out_specs=[
            pl.BlockSpec((tm, tn_h), lambda j: (0, j)),
            pl.BlockSpec((n_heads, tm), lambda j: (0, 0)),
            pl.BlockSpec((tn_h, d), lambda j: (j, 0)),
            pl.BlockSpec((n_heads, d), lambda j: (0, 0)),
        ],
        out_shape=out_shape + [jax.ShapeDtypeStruct((n, d), BF16),
                               jax.ShapeDtypeStruct((n_heads, d), BF16)],
        scratch_shapes=[pltpu.VMEM((tm, d), BF16)],
        compiler_params=pltpu.CompilerParams(
            dimension_semantics=("arbitrary",), vmem_limit_bytes=VMEM_LIMIT),
        name="inproj_head",
    )(x2d, norm_w, w_t, w_t)

    any_spec = pl.BlockSpec(memory_space=pl.ANY)
    return pl.pallas_call(
        functools.partial(_inproj_kernel, row_chunk=128),
        grid=(t // tm - 1, n // tn),
        in_specs=[
            pl.BlockSpec((tm, d), lambda i, j: (i + 1, 0)),
            pl.BlockSpec((1, d), lambda i, j: (0, 0)),
            pl.BlockSpec((tn, d), lambda i, j: (j, 0)),
            pl.BlockSpec((n_heads, d), lambda i, j: (0, 0)),
            any_spec, any_spec,
        ],
        out_specs=[
            pl.BlockSpec((tm, tn), lambda i, j: (i + 1, j)),
            pl.BlockSpec((n_heads, tm), lambda i, j: (0, i + 1)),
        ],
        out_shape=out_shape,
        input_output_aliases={4: 0, 5: 1},
        scratch_shapes=[pltpu.VMEM((tm, d), BF16)],
        compiler_params=pltpu.CompilerParams(
            dimension_semantics=("arbitrary", "arbitrary"),
            vmem_limit_bytes=VMEM_LIMIT),
        name="inproj",
    )(x2d, norm_w, w_bf, w_dt_bf, proj, dt_t)


def _gmlp_rows(u_ref, v_ref, z_ref, lnw_ref, lnb_ref, ws_ref, bst_ref, y_ref):
    tb, e = u_ref.shape
    l = GMLP_CHUNK
    gd = e // GMLP_GROUPS
    row = lax.broadcasted_iota(jnp.int32, (l, l), 0)
    col = lax.broadcasted_iota(jnp.int32, (l, l), 1)
    causal = col <= row
    wm = [jnp.where(causal, ws_ref[g], 0.0).astype(BF16) for g in range(GMLP_GROUPS)]
    lnw = lnw_ref[...]
    lnb = lnb_ref[...]

    for ci in range(tb // l):
        rows = slice(ci * l, (ci + 1) * l)
        v = v_ref[rows, :].astype(F32)
        mu = jnp.mean(v, axis=-1, keepdims=True)
        dv = v - mu
        var = jnp.mean(dv * dv, axis=-1, keepdims=True)
        vn = (dv * lax.rsqrt(var + EPS) * lnw + lnb).astype(BF16)
        for g in range(GMLP_GROUPS):
            cs = slice(g * gd, (g + 1) * gd)
            mixed = jnp.dot(wm[g], vn[:, cs], preferred_element_type=F32) + bst_ref[:, g:g + 1]
            gate = _silu(z_ref[rows, cs]) * u_ref[rows, cs]
            y_ref[rows, cs] = gate * mixed.astype(BF16)


def _ssd_kernel(zx_ref, dtt_ref, dtb_ref, alog_ref, shift_ref, cw_ref, cbias_ref, dskip_ref,
                nw_ref, e3_ref, wb_ref, wo_ref, y_ref, wbbf_ref, wobf_ref, xx_ref, xc_ref, ht_ref,
                we_ref):
    wbbf_ref[...] = wb_ref[...].astype(wbbf_ref.dtype)
    wobf_ref[...] = wo_ref[...].astype(wobf_ref.dtype)

    zb_ref = zx_ref.at[:, 0:y_ref.shape[1]]
    xbc_ref = zx_ref.at[:, y_ref.shape[1]:]
    c = pl.program_id(1)
    l = SSD_CHUNK
    n_heads = dtt_ref.shape[0]
    width = zb_ref.shape[1]
    conv_k = cw_ref.shape[0]
    n = SSD_STATE
    g_n = SSD_GROUPS
    gw = width // g_n
    hpg = gw // SSD_HEAD_DIM
    tail = xx_ref.shape[0] - l

    @pl.when(c == 0)
    def _init():
        xx_ref[0:tail, :] = jnp.zeros((tail, xx_ref.shape[1]), BF16)
        ht_ref[...] = jnp.zeros(ht_ref.shape, F32)

    def chunk(ci, carry):
        rows = pl.ds(pl.multiple_of(ci * l, l), l)
        cur = xbc_ref[rows, :]
        xx_ref[tail:tail + l, :] = cur
        shifted = jnp.dot(shift_ref[...], xx_ref[...], preferred_element_type=F32)
        acc = cbias_ref[...]
        for k in range(conv_k - 1):
            acc = acc + shifted[k * l:(k + 1) * l, :] * cw_ref[k:k + 1, :]
        acc = acc + cur.astype(F32) * cw_ref[conv_k - 1:conv_k, :]
        xc_ref[...] = _silu(acc)
        xx_ref[0:tail, :] = cur[l - tail:l, :]

        row = lax.broadcasted_iota(jnp.int32, (l, l), 0)
        col = lax.broadcasted_iota(jnp.int32, (l, l), 1)
        causal = col <= row
        triu = (row <= col).astype(BF16)
        dt_t = _softplus(dtt_ref[:, rows] + dtb_ref[...])
        adt_t = dt_t * (-jnp.exp(alog_ref[...]))
        a_row = None
        for term in _split_bf16(adt_t, SPLIT_TERMS):
            part = jnp.dot(term.astype(BF16), triu, preferred_element_type=F32)
            a_row = part if a_row is None else a_row + part
        a_last = a_row[:, l - 1:l]
        ap = a_row - jnp.log(dt_t)

        zpad = jnp.zeros((LANES - n_heads, l), F32)
        to_col = lambda v: jnp.concatenate([v, zpad], axis=0).T
        a_col = to_col(a_row)

        def pack(v):
            hi, mid, lo = _split_bf16(v, SPLIT_TERMS)
            return (hi + pltpu.roll(mid, n_heads, axis=1)
                    + pltpu.roll(lo, 2 * n_heads, axis=1)).astype(BF16)

        e3 = e3_ref[...]
        we_ref[0:l, :] = jnp.dot(pack(to_col(dt_t * jnp.exp(a_last - a_row))), e3,
                                 preferred_element_type=F32)
        we_ref[l:2 * l, :] = jnp.dot(pack(to_col(jnp.exp(a_row))), e3, preferred_element_type=F32)

        lane_g = lax.broadcasted_iota(jnp.int32, (l, gw), 1)
        for g in range(g_n):
            gs = slice(g * gw, (g + 1) * gw)
            b_g = xc_ref[:, width + g * n: width + (g + 1) * n].astype(BF16)
            c_g = xc_ref[:, width + (g_n + g) * n: width + (g_n + g + 1) * n].astype(BF16)
            cb = lax.dot_general(c_g, b_g, NT_DIMS, preferred_element_type=F32)
            cb = jnp.where(causal, cb, 0.0)
            xs_g = xc_ref[:, gs]
            xs_bf = xs_g.astype(BF16)

            scores = []
            xbd = []
            for k in range(hpg):
                h = g * hpg + k
                seg = a_col[:, h:h + 1] - ap[h:h + 1, :]
                scores.append((cb * jnp.exp(jnp.where(causal, seg, 0.0))).astype(BF16))
                in_head = (lane_g >= k * SSD_HEAD_DIM) & (lane_g < (k + 1) * SSD_HEAD_DIM)
                xbd.append(jnp.where(in_head, xs_bf, jnp.zeros_like(xs_bf)))
            s_cat = jnp.concatenate(scores, axis=1)
            x_bd = jnp.concatenate(xbd, axis=0)
            y_diag = jnp.dot(s_cat, x_bd, preferred_element_type=F32)

            h_prev = ht_ref[g]
            y_off = jnp.dot(c_g, h_prev.astype(BF16), preferred_element_type=F32) * we_ref[l:2 * l, gs]
            x_dec = (xs_g * we_ref[0:l, gs]).astype(BF16)
            st = lax.dot_general(b_g, x_dec, TN_DIMS, preferred_element_type=F32)
            ht_ref[g] = h_prev * we_ref[2 * l - 1:2 * l, gs] + st

            y = y_diag + y_off + dskip_ref[:, gs] * xs_g
            y = y * _silu(zb_ref[rows, gs]).astype(F32)
            ms = jnp.mean(y * y, axis=-1, keepdims=True)
            y_ref[rows, gs] = (y * lax.rsqrt(ms + EPS) * nw_ref[:, gs]).astype(y_ref.dtype)
        return carry

    lax.fori_loop(0, zb_ref.shape[0] // l, chunk, 0)


def _ssd(proj, dt_t, dtb_col, alog_col, shift, conv_w, conv_b, dskip_e, norm_w, e3, w_branch2d,
         w_out, batch, seq, width):
    t = proj.shape[0]
    n_heads = dt_t.shape[0]
    l = SSD_CHUNK
    conv_k, conv_dim = conv_w.shape
    assert conv_dim == width + 2 * SSD_GROUPS * SSD_STATE
    assert width % SSD_GROUPS == 0 and conv_k - 1 <= l and SPLIT_TERMS * n_heads <= LANES
    assert shift.shape == ((conv_k - 1) * l, CONV_TAIL + l) and conv_k - 1 <= CONV_TAIL
    zx_w = width + conv_dim
    zx_blk = (3 * width) // zx_w
    assert zx_blk * zx_w == 3 * width
    tb = SSD_TB
    nb = seq // tb
    assert seq % tb == 0 and tb % l == 0
    n_steps = batch * nb
    wb_rows, wo_rows = w_branch2d.shape[0] // n_steps, w_out.shape[0] // n_steps
    assert wb_rows * n_steps == w_branch2d.shape[0] and wo_rows * n_steps == w_out.shape[0]
    assert wb_rows % (2 * SUBLANES) == 0 and wo_rows % (2 * SUBLANES) == 0
    tok = lambda b, c: b * nb + c
    const = lambda b, c: (0, 0)
    slab = lambda rows, arr: pl.BlockSpec((rows, arr.shape[1]), lambda b, c: (tok(b, c), 0))
    return pl.pallas_call(
        _ssd_kernel,
        grid=(batch, nb),
        in_specs=[
            pl.BlockSpec((tb, zx_w), lambda b, c: (tok(b, c), zx_blk)),
            pl.BlockSpec((n_heads, tb), lambda b, c: (0, tok(b, c))),
            pl.BlockSpec((n_heads, 1), const),
            pl.BlockSpec((n_heads, 1), const),
            pl.BlockSpec(shift.shape, const),
            pl.BlockSpec((conv_k, conv_dim), const),
            pl.BlockSpec((1, conv_dim), const),
            pl.BlockSpec((1, width), const),
            pl.BlockSpec((1, width), const),
            pl.BlockSpec((LANES, width), const),
            slab(wb_rows, w_branch2d),
            slab(wo_rows, w_out),
        ],
        out_specs=[
            pl.BlockSpec((tb, width), lambda b, c: (tok(b, c), 0)),
            slab(wb_rows, w_branch2d),
            slab(wo_rows, w_out),
        ],
        out_shape=[
            jax.ShapeDtypeStruct((t, width), BF16),
            jax.ShapeDtypeStruct(w_branch2d.shape, BF16),
            jax.ShapeDtypeStruct(w_out.shape, BF16),
        ],
        scratch_shapes=[
            pltpu.VMEM((CONV_TAIL + l, conv_dim), BF16),
            pltpu.VMEM((l, conv_dim), F32),
            pltpu.VMEM((SSD_GROUPS, SSD_STATE, width // SSD_GROUPS), F32),
            pltpu.VMEM((2 * l, width), F32),
        ],
        compiler_params=pltpu.CompilerParams(
            dimension_semantics=("arbitrary", "arbitrary"), vmem_limit_bytes=VMEM_LIMIT),
        name="ssd",
    )(proj, dt_t, dtb_col, alog_col, shift, conv_w, conv_b, dskip_e, norm_w, e3, w_branch2d, w_out)


def _merge_kernel(uvz_ref, lnw_ref, lnb_ref, ws_ref, bst_ref, yb_ref, gl_ref, x_ref,
                  bg_ref, wb_ref, wo_ref, fnw_ref, o_ref, ya_ref, *, final):
    d = x_ref.shape[1]
    e = yb_ref.shape[1]
    u_ref, v_ref, za_ref = (uvz_ref.at[:, k * e:(k + 1) * e] for k in range(3))
    bd1 = jnp.dot(yb_ref[...], wb_ref[e:2 * e, :], preferred_element_type=F32)
    _gmlp_rows(u_ref, v_ref, za_ref, lnw_ref, lnb_ref, ws_ref, bst_ref, ya_ref)
    bd0 = jnp.dot(ya_ref[...], wb_ref[0:e, :], preferred_element_type=F32)
    g0 = _sigmoid(gl_ref[:, 0:d].astype(F32) + bg_ref[0:1, :])
    g1 = _sigmoid(gl_ref[:, d:2 * d].astype(F32) + bg_ref[1:2, :])
    merged = (g0 * bd0 + g1 * bd1).astype(BF16)
    xn = x_ref[...] + jnp.dot(merged, wo_ref[...], preferred_element_type=F32)
    if final:
        ms = jnp.mean(xn * xn, axis=-1, keepdims=True)
        xn = xn * lax.rsqrt(ms + EPS) * fnw_ref[...]
    o_ref[...] = xn


def _merge(proj, ln_w, ln_b, w_s, b_s_t, y_b, x2d, b_gate, w_branch, w_out, final_norm_w, final):
    t, d = x2d.shape
    e = y_b.shape[1]
    tm = MERGE_TM
    g, l, _ = w_s.shape
    gate_blk = (proj.shape[1] - N_BRANCH * d) // (N_BRANCH * d)
    assert t % tm == 0 and gate_blk * N_BRANCH * d == proj.shape[1] - N_BRANCH * d
    assert l == GMLP_CHUNK and g == GMLP_GROUPS and tm % l == 0
    resident = dict(pipeline_mode=pl.Buffered(1))
    return pl.pallas_call(
        functools.partial(_merge_kernel, final=final),
        grid=(t // tm,),
        in_specs=[
            pl.BlockSpec((tm, 3 * e), lambda i: (i, 0)),
            pl.BlockSpec((1, e), lambda i: (0, 0)),
            pl.BlockSpec((1, e), lambda i: (0, 0)),
            pl.BlockSpec((g, l, l), lambda i: (0, 0, 0)),
            pl.BlockSpec((l, g), lambda i: (0, 0)),
            pl.BlockSpec((tm, e), lambda i: (i, 0)),
            pl.BlockSpec((tm, N_BRANCH * d), lambda i: (i, gate_blk)),
            pl.BlockSpec((tm, d), lambda i: (i, 0)),
            pl.BlockSpec((N_BRANCH, d), lambda i: (0, 0)),
            pl.BlockSpec((N_BRANCH * e, d), lambda i: (0, 0), **resident),
            pl.BlockSpec((d, d), lambda i: (0, 0), **resident),
            pl.BlockSpec((1, d), lambda i: (0, 0)),
        ],
        out_specs=pl.BlockSpec((tm, d), lambda i: (i, 0)),
        out_shape=jax.ShapeDtypeStruct((t, d), F32),
        scratch_shapes=[pltpu.VMEM((tm, e), BF16)],
        compiler_params=pltpu.CompilerParams(
            dimension_semantics=("arbitrary",), vmem_limit_bytes=VMEM_LIMIT),
        name="merge",
    )(proj, ln_w, ln_b, w_s, b_s_t, y_b, proj, x2d, b_gate, w_branch, w_out, final_norm_w)


def kernel(x, norm_w, w_in, b_gate, ln_v_w, ln_v_b, w_spatial, b_spatial, conv_w, conv_b,
           dt_bias, a_log, d_skip, ssm_norm_w, w_branch, w_out, final_norm_w):
    batch, seq, d = x.shape
    depth = norm_w.shape[0]
    width = ln_v_w.shape[1]
    n_heads = dt_bias.shape[1]
    conv_k, conv_dim = conv_w.shape[1:]
    assert width == d and n_heads * SSD_HEAD_DIM == width
    o_dt = 4 * width + conv_dim
    t = batch * seq
    x2d = x.reshape(t, d)

    head_of_col = jnp.arange(width, dtype=jnp.int32) // SSD_HEAD_DIM
    e_rows = jnp.arange(LANES, dtype=jnp.int32)
    e3 = ((e_rows[:, None] % n_heads == head_of_col[None, :])
          & (e_rows[:, None] < SPLIT_TERMS * n_heads)).astype(BF16)
    tap = jnp.arange((conv_k - 1) * SSD_CHUNK, dtype=jnp.int32)
    src = CONV_TAIL + tap % SSD_CHUNK - (conv_k - 1) + tap // SSD_CHUNK
    shift = (src[:, None]
             == jnp.arange(CONV_TAIL + SSD_CHUNK, dtype=jnp.int32)[None, :]).astype(BF16)

    for layer in range(depth):
        proj, dt_t = _inproj(x2d, norm_w[layer][None, :], jnp.swapaxes(w_in[layer], 0, 1), o_dt,
                             n_heads)

        y_b, w_branch_bf, w_out_bf = _ssd(
            proj, dt_t, dt_bias[layer][:, None], a_log[layer][:, None], shift, conv_w[layer],
            conv_b[layer][None, :], jnp.repeat(d_skip[layer], SSD_HEAD_DIM)[None, :],
            ssm_norm_w[layer][None, :], e3, w_branch[layer].reshape(N_BRANCH * width, d),
            w_out[layer], batch, seq, width)

        x2d = _merge(proj, ln_v_w[layer][None, :], ln_v_b[layer][None, :], w_spatial[layer],
                     b_spatial[layer].T, y_b, x2d, b_gate[layer], w_branch_bf, w_out_bf,
                     final_norm_w[None, :], final=(layer == depth - 1))
    return x2d.reshape(batch, seq, d)
```

```python
import functools

import jax
import jax.numpy as jnp
from jax import lax
from jax.experimental import pallas as pl
from jax.experimental.pallas import tpu as pltpu

F32 = jnp.float32
BF16 = jnp.bfloat16

EPS = 1e-5
GMLP_GROUPS = 8
GMLP_CHUNK = 128
SSD_HEAD_DIM = 64
SSD_GROUPS = 8
SSD_STATE = 128
SSD_CHUNK = 128
N_BRANCH = 2

LANES = 128
SUBLANES = 8
SPLIT_TERMS = 3
CONV_TAIL = 2 * SUBLANES

INPROJ_HEAD_TM = 2048
INPROJ_HEAD_TN = 512
INPROJ_TM = 1024
INPROJ_TN = 2048
SSD_TB = 512
MERGE_TM = 256
VMEM_LIMIT = 60000 * 1024

NT_DIMS = (((1,), (1,)), ((), ()))
TN_DIMS = (((0,), (0,)), ((), ()))


def _silu(x):
    hx = 0.5 * x
    return hx + hx * jnp.tanh(hx)


def _sigmoid(x):
    return 1.0 / (1.0 + jnp.exp(-x))


def _softplus(x):
    return jnp.maximum(x, 0.0) + jnp.log1p(jnp.exp(-jnp.abs(x)))


def _split_bf16(v, n_terms):
    terms = []
    r = v
    for _ in range(n_terms):
        t = r.astype(BF16).astype(F32)
        terms.append(t)
        r = r - t
    return terms


def _rmsnorm_rows(x_ref, nw_ref, h_ref, row_chunk):
    def body(r, carry):
        r0 = pl.multiple_of(r * row_chunk, row_chunk)
        x = x_ref[pl.ds(r0, row_chunk), :]
        ms = jnp.mean(x * x, axis=-1, keepdims=True)
        h = x * lax.rsqrt(ms + EPS) * nw_ref[...]
        h_ref[pl.ds(r0, row_chunk), :] = h.astype(BF16)
        return carry
    lax.fori_loop(0, x_ref.shape[0] // row_chunk, body, 0)


def _inproj_head_kernel(x_ref, nw_ref, w_ref, wdt_ref, proj_ref, dtt_ref, wbf_ref, wdtbf_ref,
                        h_ref, *, row_chunk):
    @pl.when(pl.program_id(0) == 0)
    def _norm():
        _rmsnorm_rows(x_ref, nw_ref, h_ref, row_chunk)
        wdt = wdt_ref[...].astype(BF16)
        wdtbf_ref[...] = wdt
        dtt_ref[...] = lax.dot_general(wdt, h_ref[...], NT_DIMS, preferred_element_type=F32)

    w = w_ref[...].astype(BF16)
    wbf_ref[...] = w
    proj_ref[...] = lax.dot_general(h_ref[...], w, NT_DIMS,
                                    preferred_element_type=F32).astype(proj_ref.dtype)


def _inproj_kernel(x_ref, nw_ref, w_ref, wdt_ref, proj_in_ref, dtt_in_ref, proj_ref, dtt_ref,
                   h_ref, *, row_chunk):
    del proj_in_ref, dtt_in_ref

    @pl.when(pl.program_id(1) == 0)
    def _norm():
        _rmsnorm_rows(x_ref, nw_ref, h_ref, row_chunk)
        dtt_ref[...] = lax.dot_general(wdt_ref[...], h_ref[...], NT_DIMS,
                                       preferred_element_type=F32)

    proj_ref[...] = lax.dot_general(h_ref[...], w_ref[...], NT_DIMS,
                                    preferred_element_type=F32).astype(proj_ref.dtype)


def _inproj(x2d, norm_w, w_t, o_dt, n_heads):
    t, d = x2d.shape
    n = w_t.shape[0] - n_heads
    tm, tn, tm_h, tn_h = INPROJ_TM, INPROJ_TN, INPROJ_HEAD_TM, INPROJ_HEAD_TN
    skip = tm_h // tm
    assert t % tm == 0 and n % tn == 0 and n % tn_h == 0 and tm_h % tm == 0 and t > tm_h
    assert o_dt % tn_h == 0 and o_dt % n_heads == 0
    out_shape = [jax.ShapeDtypeStruct((t, n), BF16), jax.ShapeDtypeStruct((n_heads, t), F32)]
    w_row = lambda j: (
        pl.multiple_of(j * tn_h + jnp.where(j * tn_h >= o_dt, n_heads, 0), n_heads), 0)
    proj, dt_t, w_bf, w_dt_bf = pl.pallas_call(
        functools.partial(_inproj_head_kernel, row_chunk=128),
        grid=(n // tn_h,),
        in_specs=[
            pl.BlockSpec((tm_h, d), lambda j: (0, 0), pipeline_mode=pl.Buffered(1)),
            pl.BlockSpec((1, d), lambda j: (0, 0)),
            pl.BlockSpec((pl.Element(tn_h), pl.Element(d)), w_row),
            pl.BlockSpec((n_heads, d), lambda j: (o_dt // n_heads, 0)),
        ],
        out_specs=[
            pl.BlockSpec((tm_h, tn_h), lambda j: (0, j)),
            pl.BlockSpec((n_heads, tm_h), lambda j: (0, 0)),
            pl.BlockSpec((tn_h, d), lambda j: (j, 0)),
            pl.BlockSpec((n_heads, d), lambda j: (0, 0)),
        ],
        out_shape=out_shape + [jax.ShapeDtypeStruct((n, d), BF16),
                               jax.ShapeDtypeStruct((n_heads, d), BF16)],
        scratch_shapes=[pltpu.VMEM((tm_h, d), BF16)],
        compiler_params=pltpu.CompilerParams(
            dimension_semantics=("arbitrary",), vmem_limit_bytes=VMEM_LIMIT),
        name="inproj_head",
    )(x2d, norm_w, w_t, w_t)

    any_spec = pl.BlockSpec(memory_space=pl.ANY)
    return pl.pallas_call(
        functools.partial(_inproj_kernel, row_chunk=128),
        grid=(t // tm - skip, n // tn),
        in_specs=[
            pl.BlockSpec((tm, d), lambda i, j: (i + skip, 0)),
            pl.BlockSpec((1, d), lambda i, j: (0, 0)),
            pl.BlockSpec((tn, d), lambda i, j: (j, 0)),
            pl.BlockSpec((n_heads, d), lambda i, j: (0, 0)),
            any_spec, any_spec,
        ],
        out_specs=[
            pl.BlockSpec((tm, tn), lambda i, j: (i + skip, j)),
            pl.BlockSpec((n_heads, tm), lambda i, j: (0, i + skip)),
        ],
        out_shape=out_shape,
        input_output_aliases={4: 0, 5: 1},
        scratch_shapes=[pltpu.VMEM((tm, d), BF16)],
        compiler_params=pltpu.CompilerParams(
            dimension_semantics=("arbitrary", "arbitrary"),
            vmem_limit_bytes=VMEM_LIMIT),
        name="inproj",
    )(x2d, norm_w, w_bf, w_dt_bf, proj, dt_t)


def _gmlp_rows(u_ref, v_ref, z_ref, lnw_ref, lnb_ref, ws_ref, bst_ref, y_ref):
    tb, e = u_ref.shape
    l = GMLP_CHUNK
    gd = e // GMLP_GROUPS
    row = lax.broadcasted_iota(jnp.int32, (l, l), 0)
    col = lax.broadcasted_iota(jnp.int32, (l, l), 1)
    causal = col <= row
    wm = [jnp.where(causal, ws_ref[g], 0.0).astype(BF16) for g in range(GMLP_GROUPS)]
    lnw = lnw_ref[...]
    lnb = lnb_ref[...]

    for ci in range(tb // l):
        rows = slice(ci * l, (ci + 1) * l)
        v = v_ref[rows, :].astype(F32)
        mu = jnp.mean(v, axis=-1, keepdims=True)
        dv = v - mu
        var = jnp.mean(dv * dv, axis=-1, keepdims=True)
        vn = (dv * lax.rsqrt(var + EPS) * lnw + lnb).astype(BF16)
        for g in range(GMLP_GROUPS):
            cs = slice(g * gd, (g + 1) * gd)
            mixed = jnp.dot(wm[g], vn[:, cs], preferred_element_type=F32) + bst_ref[:, g:g + 1]
            gate = _silu(z_ref[rows, cs]) * u_ref[rows, cs]
            y_ref[rows, cs] = gate * mixed.astype(BF16)


def _ssd_kernel(zx_ref, dtt_ref, dtb_ref, alog_ref, shift_ref, cw_ref, cbias_ref, dskip_ref,
                nw_ref, e3_ref, wb_ref, wo_ref, y_ref, wbbf_ref, wobf_ref, xx_ref, xc_ref, ht_ref,
                we_ref):
    wbbf_ref[...] = wb_ref[...].astype(wbbf_ref.dtype)
    wobf_ref[...] = wo_ref[...].astype(wobf_ref.dtype)

    zb_ref = zx_ref.at[:, 0:y_ref.shape[1]]
    xbc_ref = zx_ref.at[:, y_ref.shape[1]:]
    c = pl.program_id(1)
    l = SSD_CHUNK
    n_heads = dtt_ref.shape[0]
    width = zb_ref.shape[1]
    conv_k = cw_ref.shape[0]
    n = SSD_STATE
    g_n = SSD_GROUPS
    gw = width // g_n
    hpg = gw // SSD_HEAD_DIM
    tail = xx_ref.shape[0] - l

    @pl.when(c == 0)
    def _init():
        xx_ref[0:tail, :] = jnp.zeros((tail, xx_ref.shape[1]), BF16)
        ht_ref[...] = jnp.zeros(ht_ref.shape, F32)

    def chunk(ci, carry):
        rows = pl.ds(pl.multiple_of(ci * l, l), l)
        cur = xbc_ref[rows, :]
        xx_ref[tail:tail + l, :] = cur
        shifted = jnp.dot(shift_ref[...], xx_ref[...], preferred_element_type=F32)
        acc = cbias_ref[...]
        for k in range(conv_k - 1):
            acc = acc + shifted[k * l:(k + 1) * l, :] * cw_ref[k:k + 1, :]
        acc = acc + cur.astype(F32) * cw_ref[conv_k - 1:conv_k, :]
        xc_ref[...] = _silu(acc)
        xx_ref[0:tail, :] = cur[l - tail:l, :]

        row = lax.broadcasted_iota(jnp.int32, (l, l), 0)
        col = lax.broadcasted_iota(jnp.int32, (l, l), 1)
        causal = col <= row
        triu = (row <= col).astype(BF16)
        dt_t = _softplus(dtt_ref[:, rows] + dtb_ref[...])
        adt_t = dt_t * (-jnp.exp(alog_ref[...]))
        a_row = None
        for term in _split_bf16(adt_t, SPLIT_TERMS):
            part = jnp.dot(term.astype(BF16), triu, preferred_element_type=F32)
            a_row = part if a_row is None else a_row + part
        a_last = a_row[:, l - 1:l]
        ap = a_row - jnp.log(dt_t)

        zpad = jnp.zeros((LANES - n_heads, l), F32)
        to_col = lambda v: jnp.concatenate([v, zpad], axis=0).T
        a_col = to_col(a_row)

        def pack(v):
            hi, mid, lo = _split_bf16(v, SPLIT_TERMS)
            return (hi + pltpu.roll(mid, n_heads, axis=1)
                    + pltpu.roll(lo, 2 * n_heads, axis=1)).astype(BF16)

        e3 = e3_ref[...]
        we_ref[0:l, :] = jnp.dot(pack(to_col(dt_t * jnp.exp(a_last - a_row))), e3,
                                 preferred_element_type=F32)
        we_ref[l:2 * l, :] = jnp.dot(pack(to_col(jnp.exp(a_row))), e3, preferred_element_type=F32)

        lane_g = lax.broadcasted_iota(jnp.int32, (l, gw), 1)
        for g in range(g_n):
            gs = slice(g * gw, (g + 1) * gw)
            b_g = xc_ref[:, width + g * n: width + (g + 1) * n].astype(BF16)
            c_g = xc_ref[:, width + (g_n + g) * n: width + (g_n + g + 1) * n].astype(BF16)
            cb = lax.dot_general(c_g, b_g, NT_DIMS, preferred_element_type=F32)
            cb = jnp.where(causal, cb, 0.0)
            xs_g = xc_ref[:, gs]
            xs_bf = xs_g.astype(BF16)

            scores = []
            xbd = []
            for k in range(hpg):
                h = g * hpg + k
                seg = a_col[:, h:h + 1] - ap[h:h + 1, :]
                scores.append((cb * jnp.exp(jnp.where(causal, seg, 0.0))).astype(BF16))
                in_head = (lane_g >= k * SSD_HEAD_DIM) & (lane_g < (k + 1) * SSD_HEAD_DIM)
                xbd.append(jnp.where(in_head, xs_bf, jnp.zeros_like(xs_bf)))
            s_cat = jnp.concatenate(scores, axis=1)
            x_bd = jnp.concatenate(xbd, axis=0)
            y_diag = jnp.dot(s_cat, x_bd, preferred_element_type=F32)

            h_prev = ht_ref[g]
            y_off = jnp.dot(c_g, h_prev.astype(BF16), preferred_element_type=F32) * we_ref[l:2 * l, gs]
            x_dec = (xs_g * we_ref[0:l, gs]).astype(BF16)
            st = lax.dot_general(b_g, x_dec, TN_DIMS, preferred_element_type=F32)
            ht_ref[g] = h_prev * we_ref[2 * l - 1:2 * l, gs] + st

            y = y_diag + y_off + dskip_ref[:, gs] * xs_g
            y = y * _silu(zb_ref[rows, gs]).astype(F32)
            ms = jnp.mean(y * y, axis=-1, keepdims=True)
            y_ref[rows, gs] = (y * lax.rsqrt(ms + EPS) * nw_ref[:, gs]).astype(y_ref.dtype)
        return carry

    lax.fori_loop(0, zb_ref.shape[0] // l, chunk, 0)


def _ssd(proj, dt_t, dtb_col, alog_col, shift, conv_w, conv_b, dskip_e, norm_w, e3, w_branch2d,
         w_out, batch, seq, width):
    t = proj.shape[0]
    n_heads = dt_t.shape[0]
    l = SSD_CHUNK
    conv_k, conv_dim = conv_w.shape
    assert conv_dim == width + 2 * SSD_GROUPS * SSD_STATE
    assert width % SSD_GROUPS == 0 and conv_k - 1 <= l and SPLIT_TERMS * n_heads <= LANES
    assert shift.shape == ((conv_k - 1) * l, CONV_TAIL + l) and conv_k - 1 <= CONV_TAIL
    zx_w = width + conv_dim
    zx_blk = (3 * width) // zx_w
    assert zx_blk * zx_w == 3 * width
    tb = SSD_TB
    nb = seq // tb
    assert seq % tb == 0 and tb % l == 0
    n_steps = batch * nb
    wb_rows, wo_rows = w_branch2d.shape[0] // n_steps, w_out.shape[0] // n_steps
    assert wb_rows * n_steps == w_branch2d.shape[0] and wo_rows * n_steps == w_out.shape[0]
    assert wb_rows % (2 * SUBLANES) == 0 and wo_rows % (2 * SUBLANES) == 0
    tok = lambda b, c: b * nb + c
    const = lambda b, c: (0, 0)
    slab = lambda rows, arr: pl.BlockSpec((rows, arr.shape[1]), lambda b, c: (tok(b, c), 0))
    return pl.pallas_call(
        _ssd_kernel,
        grid=(batch, nb),
        in_specs=[
            pl.BlockSpec((tb, zx_w), lambda b, c: (tok(b, c), zx_blk)),
            pl.BlockSpec((n_heads, tb), lambda b, c: (0, tok(b, c))),
            pl.BlockSpec((n_heads, 1), const),
            pl.BlockSpec((n_heads, 1), const),
            pl.BlockSpec(shift.shape, const),
            pl.BlockSpec((conv_k, conv_dim), const),
            pl.BlockSpec((1, conv_dim), const),
            pl.BlockSpec((1, width), const),
            pl.BlockSpec((1, width), const),
            pl.BlockSpec((LANES, width), const),
            slab(wb_rows, w_branch2d),
            slab(wo_rows, w_out),
        ],
        out_specs=[
            pl.BlockSpec((tb, width), lambda b, c: (tok(b, c), 0)),
            slab(wb_rows, w_branch2d),
            slab(wo_rows, w_out),
        ],
        out_shape=[
            jax.ShapeDtypeStruct((t, width), BF16),
            jax.ShapeDtypeStruct(w_branch2d.shape, BF16),
            jax.ShapeDtypeStruct(w_out.shape, BF16),
        ],
        scratch_shapes=[
            pltpu.VMEM((CONV_TAIL + l, conv_dim), BF16),
            pltpu.VMEM((l, conv_dim), F32),
            pltpu.VMEM((SSD_GROUPS, SSD_STATE, width // SSD_GROUPS), F32),
            pltpu.VMEM((2 * l, width), F32),
        ],
        compiler_params=pltpu.CompilerParams(
            dimension_semantics=("arbitrary", "arbitrary"), vmem_limit_bytes=VMEM_LIMIT),
        name="ssd",
    )(proj, dt_t, dtb_col, alog_col, shift, conv_w, conv_b, dskip_e, norm_w, e3, w_branch2d, w_out)


def _merge_kernel(uvz_ref, lnw_ref, lnb_ref, ws_ref, bst_ref, yb_ref, gl_ref, x_ref,
                  bg_ref, wb_ref, wo_ref, fnw_ref, o_ref, ya_ref, *, final):
    d = x_ref.shape[1]
    e = yb_ref.shape[1]
    u_ref, v_ref, za_ref = (uvz_ref.at[:, k * e:(k + 1) * e] for k in range(3))
    bd1 = jnp.dot(yb_ref[...], wb_ref[e:2 * e, :], preferred_element_type=F32)
    _gmlp_rows(u_ref, v_ref, za_ref, lnw_ref, lnb_ref, ws_ref, bst_ref, ya_ref)
    bd0 = jnp.dot(ya_ref[...], wb_ref[0:e, :], preferred_element_type=F32)
    g0 = _sigmoid(gl_ref[:, 0:d].astype(F32) + bg_ref[0:1, :])
    g1 = _sigmoid(gl_ref[:, d:2 * d].astype(F32) + bg_ref[1:2, :])
    merged = (g0 * bd0 + g1 * bd1).astype(BF16)
    xn = x_ref[...] + jnp.dot(merged, wo_ref[...], preferred_element_type=F32)
    if final:
        ms = jnp.mean(xn * xn, axis=-1, keepdims=True)
        xn = xn * lax.rsqrt(ms + EPS) * fnw_ref[...]
    o_ref[...] = xn


def _merge(proj, ln_w, ln_b, w_s, b_s_t, y_b, x2d, b_gate, w_branch, w_out, final_norm_w, final):
    t, d = x2d.shape
    e = y_b.shape[1]
    tm = MERGE_TM
    g, l, _ = w_s.shape
    gate_blk = (proj.shape[1] - N_BRANCH * d) // (N_BRANCH * d)
    assert t % tm == 0 and gate_blk * N_BRANCH * d == proj.shape[1] - N_BRANCH * d
    assert l == GMLP_CHUNK and g == GMLP_GROUPS and tm % l == 0
    resident = dict(pipeline_mode=pl.Buffered(1))
    return pl.pallas_call(
        functools.partial(_merge_kernel, final=final),
        grid=(t // tm,),
        in_specs=[
            pl.BlockSpec((tm, 3 * e), lambda i: (i, 0)),
            pl.BlockSpec((1, e), lambda i: (0, 0)),
            pl.BlockSpec((1, e), lambda i: (0, 0)),
            pl.BlockSpec((g, l, l), lambda i: (0, 0, 0)),
            pl.BlockSpec((l, g), lambda i: (0, 0)),
            pl.BlockSpec((tm, e), lambda i: (i, 0)),
            pl.BlockSpec((tm, N_BRANCH * d), lambda i: (i, gate_blk)),
            pl.BlockSpec((tm, d), lambda i: (i, 0)),
            pl.BlockSpec((N_BRANCH, d), lambda i: (0, 0)),
            pl.BlockSpec((N_BRANCH * e, d), lambda i: (0, 0), **resident),
            pl.BlockSpec((d, d), lambda i: (0, 0), **resident),
            pl.BlockSpec((1, d), lambda i: (0, 0)),
        ],
        out_specs=pl.BlockSpec((tm, d), lambda i: (i, 0)),
        out_shape=jax.ShapeDtypeStruct((t, d), F32),
        scratch_shapes=[pltpu.VMEM((tm, e), BF16)],
        compiler_params=pltpu.CompilerParams(
            dimension_semantics=("arbitrary",), vmem_limit_bytes=VMEM_LIMIT),
        name="merge",
    )(proj, ln_w, ln_b, w_s, b_s_t, y_b, proj, x2d, b_gate, w_branch, w_out, final_norm_w)


def kernel(x, norm_w, w_in, b_gate, ln_v_w, ln_v_b, w_spatial, b_spatial, conv_w, conv_b,
           dt_bias, a_log, d_skip, ssm_norm_w, w_branch, w_out, final_norm_w):
    batch, seq, d = x.shape
    depth = norm_w.shape[0]
    width = ln_v_w.shape[1]
    n_heads = dt_bias.shape[1]
    conv_k, conv_dim = conv_w.shape[1:]
    assert width == d and n_heads * SSD_HEAD_DIM == width
    o_dt = 4 * width + conv_dim
    t = batch * seq
    x2d = x.reshape(t, d)

    head_of_col = jnp.arange(width, dtype=jnp.int32) // SSD_HEAD_DIM
    e_rows = jnp.arange(LANES, dtype=jnp.int32)
    e3 = ((e_rows[:, None] % n_heads == head_of_col[None, :])
          & (e_rows[:, None] < SPLIT_TERMS * n_heads)).astype(BF16)
    tap = jnp.arange((conv_k - 1) * SSD_CHUNK, dtype=jnp.int32)
    src = CONV_TAIL + tap % SSD_CHUNK - (conv_k - 1) + tap // SSD_CHUNK
    shift = (src[:, None]
             == jnp.arange(CONV_TAIL + SSD_CHUNK, dtype=jnp.int32)[None, :]).astype(BF16)

    for layer in range(depth):
        proj, dt_t = _inproj(x2d, norm_w[layer][None, :], jnp.swapaxes(w_in[layer], 0, 1), o_dt,
                             n_heads)

        y_b, w_branch_bf, w_out_bf = _ssd(
            proj, dt_t, dt_bias[layer][:, None], a_log[layer][:, None], shift, conv_w[layer],
            conv_b[layer][None, :], jnp.repeat(d_skip[layer], SSD_HEAD_DIM)[None, :],
            ssm_norm_w[layer][None, :], e3, w_branch[layer].reshape(N_BRANCH * width, d),
            w_out[layer], batch, seq, width)

        x2d = _merge(proj, ln_v_w[layer][None, :], ln_v_b[layer][None, :], w_spatial[layer],
                     b_spatial[layer].T, y_b, x2d, b_gate[layer], w_branch_bf, w_out_bf,
                     final_norm_w[None, :], final=(layer == depth - 1))
    return x2d.reshape(batch, seq, d)
```

```python
import functools

import jax
import jax.numpy as jnp
from jax import lax
from jax.experimental import pallas as pl
from jax.experimental.pallas import tpu as pltpu

F32 = jnp.float32
BF16 = jnp.bfloat16

EPS = 1e-5
GMLP_GROUPS = 8
GMLP_CHUNK = 128
SSD_HEAD_DIM = 64
SSD_GROUPS = 8
SSD_STATE = 128
SSD_CHUNK = 128
N_BRANCH = 2

LANES = 128
SUBLANES = 8
SPLIT_TERMS = 3
CONV_TAIL = 2 * SUBLANES
SHIFT_SLAB = 2 * SUBLANES

INPROJ_HEAD_TM = 2048
INPROJ_HEAD_TN = 512
INPROJ_TM = 1024
INPROJ_TN = 2048
SSD_TB = 512
MERGE_TM = 256
VMEM_LIMIT = 60000 * 1024

NT_DIMS = (((1,), (1,)), ((), ()))
TN_DIMS = (((0,), (0,)), ((), ()))


def _silu(x):
    hx = 0.5 * x
    return hx + hx * jnp.tanh(hx)


def _sigmoid(x):
    return 1.0 / (1.0 + jnp.exp(-x))


def _softplus(x):
    return jnp.maximum(x, 0.0) + jnp.log1p(jnp.exp(-jnp.abs(x)))


def _split_bf16(v, n_terms):
    terms = []
    r = v
    for _ in range(n_terms):
        t = r.astype(BF16).astype(F32)
        terms.append(t)
        r = r - t
    return terms


def _rmsnorm_rows(x_ref, nw_ref, h_ref, row_chunk):
    def body(r, carry):
        r0 = pl.multiple_of(r * row_chunk, row_chunk)
        x = x_ref[pl.ds(r0, row_chunk), :]
        ms = jnp.mean(x * x, axis=-1, keepdims=True)
        h = x * lax.rsqrt(ms + EPS) * nw_ref[...]
        h_ref[pl.ds(r0, row_chunk), :] = h.astype(BF16)
        return carry
    lax.fori_loop(0, x_ref.shape[0] // row_chunk, body, 0)


def _inproj_head_kernel(x_ref, nw_ref, w_ref, wdt_ref, proj_ref, dtt_ref, wbf_ref, wdtbf_ref,
                        h_ref, *, row_chunk):
    @pl.when(pl.program_id(0) == 0)
    def _norm():
        _rmsnorm_rows(x_ref, nw_ref, h_ref, row_chunk)
        wdt = wdt_ref[...].astype(BF16)
        wdtbf_ref[...] = wdt
        dtt_ref[...] = lax.dot_general(wdt, h_ref[...], NT_DIMS, preferred_element_type=F32)

    w = w_ref[...].astype(BF16)
    wbf_ref[...] = w
    proj_ref[...] = lax.dot_general(h_ref[...], w, NT_DIMS,
                                    preferred_element_type=F32).astype(proj_ref.dtype)


def _inproj_kernel(x_ref, nw_ref, w_ref, wdt_ref, proj_in_ref, dtt_in_ref, proj_ref, dtt_ref,
                   h_ref, *, row_chunk):
    del proj_in_ref, dtt_in_ref

    @pl.when(pl.program_id(1) == 0)
    def _norm():
        _rmsnorm_rows(x_ref, nw_ref, h_ref, row_chunk)
        dtt_ref[...] = lax.dot_general(wdt_ref[...], h_ref[...], NT_DIMS,
                                       preferred_element_type=F32)

    proj_ref[...] = lax.dot_general(h_ref[...], w_ref[...], NT_DIMS,
                                    preferred_element_type=F32).astype(proj_ref.dtype)


def _inproj(x2d, norm_w, w_t, o_dt, n_heads):
    t, d = x2d.shape
    n = w_t.shape[0] - n_heads
    tm, tn, tm_h, tn_h = INPROJ_TM, INPROJ_TN, INPROJ_HEAD_TM, INPROJ_HEAD_TN
    skip = tm_h // tm
    assert t % tm == 0 and n % tn == 0 and n % tn_h == 0 and tm_h % tm == 0 and t > tm_h
    assert o_dt % tn_h == 0 and o_dt % n_heads == 0
    out_shape = [jax.ShapeDtypeStruct((t, n), BF16), jax.ShapeDtypeStruct((n_heads, t), F32)]
    w_row = lambda j: (
        pl.multiple_of(j * tn_h + jnp.where(j * tn_h >= o_dt, n_heads, 0), n_heads), 0)
    proj, dt_t, w_bf, w_dt_bf = pl.pallas_call(
        functools.partial(_inproj_head_kernel, row_chunk=128),
        grid=(n // tn_h,),
        in_specs=[
            pl.BlockSpec((tm_h, d), lambda j: (0, 0), pipeline_mode=pl.Buffered(1)),
            pl.BlockSpec((1, d), lambda j: (0, 0)),
            pl.BlockSpec((pl.Element(tn_h), pl.Element(d)), w_row),
            pl.BlockSpec((n_heads, d), lambda j: (o_dt // n_heads, 0)),
        ],
        out_specs=[
            pl.BlockSpec((tm_h, tn_h), lambda j: (0, j)),
            pl.BlockSpec((n_heads, tm_h), lambda j: (0, 0)),
            pl.BlockSpec((tn_h, d), lambda j: (j, 0)),
            pl.BlockSpec((n_heads, d), lambda j: (0, 0)),
        ],
        out_shape=out_shape + [jax.ShapeDtypeStruct((n, d), BF16),
                               jax.ShapeDtypeStruct((n_heads, d), BF16)],
        scratch_shapes=[pltpu.VMEM((tm_h, d), BF16)],
        compiler_params=pltpu.CompilerParams(
            dimension_semantics=("arbitrary",), vmem_limit_bytes=VMEM_LIMIT),
        name="inproj_head",
    )(x2d, norm_w, w_t, w_t)

    any_spec = pl.BlockSpec(memory_space=pl.ANY)
    return pl.pallas_call(
        functools.partial(_inproj_kernel, row_chunk=128),
        grid=(t // tm - skip, n // tn),
        in_specs=[
            pl.BlockSpec((tm, d), lambda i, j: (i + skip, 0)),
            pl.BlockSpec((1, d), lambda i, j: (0, 0)),
            pl.BlockSpec((tn, d), lambda i, j: (j, 0)),
            pl.BlockSpec((n_heads, d), lambda i, j: (0, 0)),
            any_spec, any_spec,
        ],
        out_specs=[
            pl.BlockSpec((tm, tn), lambda i, j: (i + skip, j)),
            pl.BlockSpec((n_heads, tm), lambda i, j: (0, i + skip)),
        ],
        out_shape=out_shape,
        input_output_aliases={4: 0, 5: 1},
        scratch_shapes=[pltpu.VMEM((tm, d), BF16)],
        compiler_params=pltpu.CompilerParams(
            dimension_semantics=("arbitrary", "arbitrary"),
            vmem_limit_bytes=VMEM_LIMIT),
        name="inproj",
    )(x2d, norm_w, w_bf, w_dt_bf, proj, dt_t)


def _gmlp_rows(u_ref, v_ref, z_ref, lnw_ref, lnb_ref, ws_ref, bst_ref, y_ref):
    tb, e = u_ref.shape
    l = GMLP_CHUNK
    gd = e // GMLP_GROUPS
    row = lax.broadcasted_iota(jnp.int32, (l, l), 0)
    col = lax.broadcasted_iota(jnp.int32, (l, l), 1)
    causal = col <= row
    wm = [jnp.where(causal, ws_ref[g], 0.0).astype(BF16) for g in range(GMLP_GROUPS)]
    lnw = lnw_ref[...]
    lnb = lnb_ref[...]

    for ci in range(tb // l):
        rows = slice(ci * l, (ci + 1) * l)
        v = v_ref[rows, :].astype(F32)
        mu = jnp.mean(v, axis=-1, keepdims=True)
        dv = v - mu
        var = jnp.mean(dv * dv, axis=-1, keepdims=True)
        vn = (dv * lax.rsqrt(var + EPS) * lnw + lnb).astype(BF16)
        for g in range(GMLP_GROUPS):
            cs = slice(g * gd, (g + 1) * gd)
            mixed = jnp.dot(wm[g], vn[:, cs], preferred_element_type=F32) + bst_ref[:, g:g + 1]
            gate = _silu(z_ref[rows, cs]) * u_ref[rows, cs]
            y_ref[rows, cs] = gate * mixed.astype(BF16)


def _ssd_kernel(zx_ref, dtt_ref, dtb_ref, alog_ref, shift_ref, cw_ref, cbias_ref, dskip_ref,
                nw_ref, e3_ref, wb_ref, wo_ref, y_ref, wbbf_ref, wobf_ref, xx_ref, xc_ref, ht_ref,
                we_ref):
    wbbf_ref[...] = wb_ref[...].astype(wbbf_ref.dtype)
    wobf_ref[...] = wo_ref[...].astype(wobf_ref.dtype)

    zb_ref = zx_ref.at[:, 0:y_ref.shape[1]]
    xbc_ref = zx_ref.at[:, y_ref.shape[1]:]
    c = pl.program_id(1)
    l = SSD_CHUNK
    n_heads = dtt_ref.shape[0]
    width = zb_ref.shape[1]
    conv_k = cw_ref.shape[0]
    n = SSD_STATE
    g_n = SSD_GROUPS
    gw = width // g_n
    hpg = gw // SSD_HEAD_DIM
    tail = xx_ref.shape[0] - l

    @pl.when(c == 0)
    def _init():
        xx_ref[0:tail, :] = jnp.zeros((tail, xx_ref.shape[1]), BF16)
        ht_ref[...] = jnp.zeros(ht_ref.shape, F32)

    def chunk(ci, carry):
        rows = pl.ds(pl.multiple_of(ci * l, l), l)
        cur = xbc_ref[rows, :]
        xx_ref[tail:tail + l, :] = cur
        shifted = jnp.dot(shift_ref[...], xx_ref[...], preferred_element_type=F32)
        n_slabs = l // SHIFT_SLAB
        tap_rows = lambda k: jnp.concatenate(
            [shifted[(s * (conv_k - 1) + k) * SHIFT_SLAB:(s * (conv_k - 1) + k + 1) * SHIFT_SLAB, :]
             for s in range(n_slabs)], axis=0)
        acc = cbias_ref[...]
        for k in range(conv_k - 1):
            acc = acc + tap_rows(k) * cw_ref[k:k + 1, :]
        acc = acc + cur.astype(F32) * cw_ref[conv_k - 1:conv_k, :]
        xc_ref[...] = _silu(acc)
        xx_ref[0:tail, :] = cur[l - tail:l, :]

        row = lax.broadcasted_iota(jnp.int32, (l, l), 0)
        col = lax.broadcasted_iota(jnp.int32, (l, l), 1)
        causal = col <= row
        triu = (row <= col).astype(BF16)
        dt_t = _softplus(dtt_ref[:, rows] + dtb_ref[...])
        adt_t = dt_t * (-jnp.exp(alog_ref[...]))
        a_row = None
        for term in _split_bf16(adt_t, SPLIT_TERMS):
            part = jnp.dot(term.astype(BF16), triu, preferred_element_type=F32)
            a_row = part if a_row is None else a_row + part
        a_last = a_row[:, l - 1:l]
        ap = a_row - jnp.log(dt_t)

        zpad = jnp.zeros((LANES - n_heads, l), F32)
        to_col = lambda v: jnp.concatenate([v, zpad], axis=0).T
        a_col = to_col(a_row)

        def pack(v):
            hi, mid, lo = _split_bf16(v, SPLIT_TERMS)
            return (hi + pltpu.roll(mid, n_heads, axis=1)
                    + pltpu.roll(lo, 2 * n_heads, axis=1)).astype(BF16)

        e3 = e3_ref[...]
        we_ref[0:l, :] = jnp.dot(pack(to_col(dt_t * jnp.exp(a_last - a_row))), e3,
                                 preferred_element_type=F32)
        we_ref[l:2 * l, :] = jnp.dot(pack(to_col(jnp.exp(a_row))), e3, preferred_element_type=F32)

        lane_g = lax.broadcasted_iota(jnp.int32, (l, gw), 1)
        for g in range(g_n):
            gs = slice(g * gw, (g + 1) * gw)
            b_g = xc_ref[:, width + g * n: width + (g + 1) * n].astype(BF16)
            c_g = xc_ref[:, width + (g_n + g) * n: width + (g_n + g + 1) * n].astype(BF16)
            cb = lax.dot_general(c_g, b_g, NT_DIMS, preferred_element_type=F32)
            cb = jnp.where(causal, cb, 0.0)
            xs_g = xc_ref[:, gs]
            xs_bf = xs_g.astype(BF16)

            scores = []
            xbd = []
            for k in range(hpg):
                h = g * hpg + k
                seg = a_col[:, h:h + 1] - ap[h:h + 1, :]
                scores.append((cb * jnp.exp(jnp.where(causal, seg, 0.0))).astype(BF16))
                in_head = (lane_g >= k * SSD_HEAD_DIM) & (lane_g < (k + 1) * SSD_HEAD_DIM)
                xbd.append(jnp.where(in_head, xs_bf, jnp.zeros_like(xs_bf)))
            s_cat = jnp.concatenate(scores, axis=1)
            x_bd = jnp.concatenate(xbd, axis=0)
            y_diag = jnp.dot(s_cat, x_bd, preferred_element_type=F32)

            h_prev = ht_ref[g]
            y_off = jnp.dot(c_g, h_prev.astype(BF16), preferred_element_type=F32) * we_ref[l:2 * l, gs]
            x_dec = (xs_g * we_ref[0:l, gs]).astype(BF16)
            st = lax.dot_general(b_g, x_dec, TN_DIMS, preferred_element_type=F32)
            ht_ref[g] = h_prev * we_ref[2 * l - 1:2 * l, gs] + st

            y = y_diag + y_off + dskip_ref[:, gs] * xs_g
            y = y * _silu(zb_ref[rows, gs]).astype(F32)
            ms = jnp.mean(y * y, axis=-1, keepdims=True)
            y_ref[rows, gs] = (y * lax.rsqrt(ms + EPS) * nw_ref[:, gs]).astype(y_ref.dtype)
        return carry

    lax.fori_loop(0, zb_ref.shape[0] // l, chunk, 0)


def _ssd(proj, dt_t, dtb_col, alog_col, shift, conv_w, conv_b, dskip_e, norm_w, e3, w_branch2d,
         w_out, batch, seq, width):
    t = proj.shape[0]
    n_heads = dt_t.shape[0]
    l = SSD_CHUNK
    conv_k, conv_dim = conv_w.shape
    assert conv_dim == width + 2 * SSD_GROUPS * SSD_STATE
    assert width % SSD_GROUPS == 0 and conv_k - 1 <= l and SPLIT_TERMS * n_heads <= LANES
    assert shift.shape == ((conv_k - 1) * l, CONV_TAIL + l) and conv_k - 1 <= CONV_TAIL
    zx_w = width + conv_dim
    zx_blk = (3 * width) // zx_w
    assert zx_blk * zx_w == 3 * width
    tb = SSD_TB
    nb = seq // tb
    assert seq % tb == 0 and tb % l == 0
    n_steps = batch * nb
    wb_rows, wo_rows = w_branch2d.shape[0] // n_steps, w_out.shape[0] // n_steps
    assert wb_rows * n_steps == w_branch2d.shape[0] and wo_rows * n_steps == w_out.shape[0]
    assert wb_rows % (2 * SUBLANES) == 0 and wo_rows % (2 * SUBLANES) == 0
    tok = lambda b, c: b * nb + c
    const = lambda b, c: (0, 0)
    slab = lambda rows, arr: pl.BlockSpec((rows, arr.shape[1]), lambda b, c: (tok(b, c), 0))
    return pl.pallas_call(
        _ssd_kernel,
        grid=(batch, nb),
        in_specs=[
            pl.BlockSpec((tb, zx_w), lambda b, c: (tok(b, c), zx_blk)),
            pl.BlockSpec((n_heads, tb), lambda b, c: (0, tok(b, c))),
            pl.BlockSpec((n_heads, 1), const),
            pl.BlockSpec((n_heads, 1), const),
            pl.BlockSpec(shift.shape, const),
            pl.BlockSpec((conv_k, conv_dim), const),
            pl.BlockSpec((1, conv_dim), const),
            pl.BlockSpec((1, width), const),
            pl.BlockSpec((1, width), const),
            pl.BlockSpec((LANES, width), const),
            slab(wb_rows, w_branch2d),
            slab(wo_rows, w_out),
        ],
        out_specs=[
            pl.BlockSpec((tb, width), lambda b, c: (tok(b, c), 0)),
            slab(wb_rows, w_branch2d),
            slab(wo_rows, w_out),
        ],
        out_shape=[
            jax.ShapeDtypeStruct((t, width), BF16),
            jax.ShapeDtypeStruct(w_branch2d.shape, BF16),
            jax.ShapeDtypeStruct(w_out.shape, BF16),
        ],
        scratch_shapes=[
            pltpu.VMEM((CONV_TAIL + l, conv_dim), BF16),
            pltpu.VMEM((l, conv_dim), F32),
            pltpu.VMEM((SSD_GROUPS, SSD_STATE, width // SSD_GROUPS), F32),
            pltpu.VMEM((2 * l, width), F32),
        ],
        compiler_params=pltpu.CompilerParams(
            dimension_semantics=("arbitrary", "arbitrary"), vmem_limit_bytes=VMEM_LIMIT),
        name="ssd",
    )(proj, dt_t, dtb_col, alog_col, shift, conv_w, conv_b, dskip_e, norm_w, e3, w_branch2d, w_out)


def _merge_kernel(uvz_ref, lnw_ref, lnb_ref, ws_ref, bst_ref, yb_ref, gl_ref, x_ref,
                  bg_ref, wb_ref, wo_ref, fnw_ref, o_ref, ya_ref, *, final):
    d = x_ref.shape[1]
    e = yb_ref.shape[1]
    u_ref, v_ref, za_ref = (uvz_ref.at[:, k * e:(k + 1) * e] for k in range(3))
    bd1 = jnp.dot(yb_ref[...], wb_ref[e:2 * e, :], preferred_element_type=F32)
    _gmlp_rows(u_ref, v_ref, za_ref, lnw_ref, lnb_ref, ws_ref, bst_ref, ya_ref)
    bd0 = jnp.dot(ya_ref[...], wb_ref[0:e, :], preferred_element_type=F32)
    g0 = _sigmoid(gl_ref[:, 0:d].astype(F32) + bg_ref[0:1, :])
    g1 = _sigmoid(gl_ref[:, d:2 * d].astype(F32) + bg_ref[1:2, :])
    merged = (g0 * bd0 + g1 * bd1).astype(BF16)
    xn = x_ref[...] + jnp.dot(merged, wo_ref[...], preferred_element_type=F32)
    if final:
        ms = jnp.mean(xn * xn, axis=-1, keepdims=True)
        xn = xn * lax.rsqrt(ms + EPS) * fnw_ref[...]
    o_ref[...] = xn


def _merge(proj, ln_w, ln_b, w_s, b_s_t, y_b, x2d, b_gate, w_branch, w_out, final_norm_w, final):
    t, d = x2d.shape
    e = y_b.shape[1]
    tm = MERGE_TM
    g, l, _ = w_s.shape
    gate_blk = (proj.shape[1] - N_BRANCH * d) // (N_BRANCH * d)
    assert t % tm == 0 and gate_blk * N_BRANCH * d == proj.shape[1] - N_BRANCH * d
    assert l == GMLP_CHUNK and g == GMLP_GROUPS and tm % l == 0
    resident = dict(pipeline_mode=pl.Buffered(1))
    return pl.pallas_call(
        functools.partial(_merge_kernel, final=final),
        grid=(t // tm,),
        in_specs=[
            pl.BlockSpec((tm, 3 * e), lambda i: (i, 0)),
            pl.BlockSpec((1, e), lambda i: (0, 0)),
            pl.BlockSpec((1, e), lambda i: (0, 0)),
            pl.BlockSpec((g, l, l), lambda i: (0, 0, 0)),
            pl.BlockSpec((l, g), lambda i: (0, 0)),
            pl.BlockSpec((tm, e), lambda i: (i, 0)),
            pl.BlockSpec((tm, N_BRANCH * d), lambda i: (i, gate_blk)),
            pl.BlockSpec((tm, d), lambda i: (i, 0)),
            pl.BlockSpec((N_BRANCH, d), lambda i: (0, 0)),
            pl.BlockSpec((N_BRANCH * e, d), lambda i: (0, 0), **resident),
            pl.BlockSpec((d, d), lambda i: (0, 0), **resident),
            pl.BlockSpec((1, d), lambda i: (0, 0)),
        ],
        out_specs=pl.BlockSpec((tm, d), lambda i: (i, 0)),
        out_shape=jax.ShapeDtypeStruct((t, d), F32),
        scratch_shapes=[pltpu.VMEM((tm, e), BF16)],
        compiler_params=pltpu.CompilerParams(
            dimension_semantics=("arbitrary",), vmem_limit_bytes=VMEM_LIMIT),
        name="merge",
    )(proj, ln_w, ln_b, w_s, b_s_t, y_b, proj, x2d, b_gate, w_branch, w_out, final_norm_w)


def kernel(x, norm_w, w_in, b_gate, ln_v_w, ln_v_b, w_spatial, b_spatial, conv_w, conv_b,
           dt_bias, a_log, d_skip, ssm_norm_w, w_branch, w_out, final_norm_w):
    batch, seq, d = x.shape
    depth = norm_w.shape[0]
    width = ln_v_w.shape[1]
    n_heads = dt_bias.shape[1]
    conv_k, conv_dim = conv_w.shape[1:]
    assert width == d and n_heads * SSD_HEAD_DIM == width
    o_dt = 4 * width + conv_dim
    t = batch * seq
    x2d = x.reshape(t, d)

    head_of_col = jnp.arange(width, dtype=jnp.int32) // SSD_HEAD_DIM
    e_rows = jnp.arange(LANES, dtype=jnp.int32)
    e3 = ((e_rows[:, None] % n_heads == head_of_col[None, :])
          & (e_rows[:, None] < SPLIT_TERMS * n_heads)).astype(BF16)
    out_row = jnp.arange((conv_k - 1) * SSD_CHUNK, dtype=jnp.int32)
    slab, in_slab = out_row // ((conv_k - 1) * SHIFT_SLAB), out_row % ((conv_k - 1) * SHIFT_SLAB)
    tap, pos = in_slab // SHIFT_SLAB, slab * SHIFT_SLAB + in_slab % SHIFT_SLAB
    src = CONV_TAIL + pos - (conv_k - 1) + tap
    shift = (src[:, None]
             == jnp.arange(CONV_TAIL + SSD_CHUNK, dtype=jnp.int32)[None, :]).astype(BF16)

    for layer in range(depth):
        proj, dt_t = _inproj(x2d, norm_w[layer][None, :], jnp.swapaxes(w_in[layer], 0, 1), o_dt,
                             n_heads)

        y_b, w_branch_bf, w_out_bf = _ssd(
            proj, dt_t, dt_bias[layer][:, None], a_log[layer][:, None], shift, conv_w[layer],
            conv_b[layer][None, :], jnp.repeat(d_skip[layer], SSD_HEAD_DIM)[None, :],
            ssm_norm_w[layer][None, :], e3, w_branch[layer].reshape(N_BRANCH * width, d),
            w_out[layer], batch, seq, width)

        x2d = _merge(proj, ln_v_w[layer][None, :], ln_v_b[layer][None, :], w_spatial[layer],
                     b_spatial[layer].T, y_b, x2d, b_gate[layer], w_branch_bf, w_out_bf,
                     final_norm_w[None, :], final=(layer == depth - 1))
    return x2d.reshape(batch, seq, d)
```

```python
import functools

import jax
import jax.numpy as jnp
from jax import lax
from jax.experimental import pallas as pl
from jax.experimental.pallas import tpu as pltpu

F32 = jnp.float32
BF16 = jnp.bfloat16

EPS = 1e-5
GMLP_GROUPS = 8
GMLP_CHUNK = 128
SSD_HEAD_DIM = 64
SSD_GROUPS = 8
SSD_STATE = 128
SSD_CHUNK = 128
N_BRANCH = 2

LANES = 128
SUBLANES = 8
SPLIT_TERMS = 3
CONV_TAIL = 2 * SUBLANES
SHIFT_SLAB = 2 * SUBLANES

INPROJ_HEAD_TM = 2048
INPROJ_HEAD_TN = 512
INPROJ_TM = 1024
INPROJ_TN = 2048
SSD_TB = 512
SSD_UNROLL = 4
MERGE_TM = 256
VMEM_LIMIT = 60000 * 1024

NT_DIMS = (((1,), (1,)), ((), ()))
TN_DIMS = (((0,), (0,)), ((), ()))


def _silu(x):
    hx = 0.5 * x
    return hx + hx * jnp.tanh(hx)


def _sigmoid(x):
    return 1.0 / (1.0 + jnp.exp(-x))


def _softplus(x):
    return jnp.maximum(x, 0.0) + jnp.log1p(jnp.exp(-jnp.abs(x)))


def _split_bf16(v, n_terms):
    terms = []
    r = v
    for _ in range(n_terms):
        t = r.astype(BF16).astype(F32)
        terms.append(t)
        r = r - t
    return terms


def _rmsnorm_rows(x_ref, nw_ref, h_ref, row_chunk):
    def body(r, carry):
        r0 = pl.multiple_of(r * row_chunk, row_chunk)
        x = x_ref[pl.ds(r0, row_chunk), :]
        ms = jnp.mean(x * x, axis=-1, keepdims=True)
        h = x * lax.rsqrt(ms + EPS) * nw_ref[...]
        h_ref[pl.ds(r0, row_chunk), :] = h.astype(BF16)
        return carry
    lax.fori_loop(0, x_ref.shape[0] // row_chunk, body, 0)


def _inproj_head_kernel(x_ref, nw_ref, w_ref, wdt_ref, proj_ref, dtt_ref, wbf_ref, wdtbf_ref,
                        h_ref, *, row_chunk):
    @pl.when(pl.program_id(0) == 0)
    def _norm():
        _rmsnorm_rows(x_ref, nw_ref, h_ref, row_chunk)
        wdt = wdt_ref[...].astype(BF16)
        wdtbf_ref[...] = wdt
        dtt_ref[...] = lax.dot_general(wdt, h_ref[...], NT_DIMS, preferred_element_type=F32)

    w = w_ref[...].astype(BF16)
    wbf_ref[...] = w
    proj_ref[...] = lax.dot_general(h_ref[...], w, NT_DIMS,
                                    preferred_element_type=F32).astype(proj_ref.dtype)


def _inproj_kernel(x_ref, nw_ref, w_ref, wdt_ref, proj_in_ref, dtt_in_ref, proj_ref, dtt_ref,
                   h_ref, *, row_chunk):
    del proj_in_ref, dtt_in_ref

    @pl.when(pl.program_id(1) == 0)
    def _norm():
        _rmsnorm_rows(x_ref, nw_ref, h_ref, row_chunk)
        dtt_ref[...] = lax.dot_general(wdt_ref[...], h_ref[...], NT_DIMS,
                                       preferred_element_type=F32)

    proj_ref[...] = lax.dot_general(h_ref[...], w_ref[...], NT_DIMS,
                                    preferred_element_type=F32).astype(proj_ref.dtype)


def _inproj(x2d, norm_w, w_t, o_dt, n_heads):
    t, d = x2d.shape
    n = w_t.shape[0] - n_heads
    tm, tn, tm_h, tn_h = INPROJ_TM, INPROJ_TN, INPROJ_HEAD_TM, INPROJ_HEAD_TN
    skip = tm_h // tm
    assert t % tm == 0 and n % tn == 0 and n % tn_h == 0 and tm_h % tm == 0 and t > tm_h
    assert o_dt % tn_h == 0 and o_dt % n_heads == 0
    out_shape = [jax.ShapeDtypeStruct((t, n), BF16), jax.ShapeDtypeStruct((n_heads, t), F32)]
    w_row = lambda j: (
        pl.multiple_of(j * tn_h + jnp.where(j * tn_h >= o_dt, n_heads, 0), n_heads), 0)
    proj, dt_t, w_bf, w_dt_bf = pl.pallas_call(
        functools.partial(_inproj_head_kernel, row_chunk=128),
        grid=(n // tn_h,),
        in_specs=[
            pl.BlockSpec((tm_h, d), lambda j: (0, 0), pipeline_mode=pl.Buffered(1)),
            pl.BlockSpec((1, d), lambda j: (0, 0)),
            pl.BlockSpec((pl.Element(tn_h), pl.Element(d)), w_row),
            pl.BlockSpec((n_heads, d), lambda j: (o_dt // n_heads, 0)),
        ],
        out_specs=[
            pl.BlockSpec((tm_h, tn_h), lambda j: (0, j)),
            pl.BlockSpec((n_heads, tm_h), lambda j: (0, 0)),
            pl.BlockSpec((tn_h, d), lambda j: (j, 0)),
            pl.BlockSpec((n_heads, d), lambda j: (0, 0)),
        ],
        out_shape=out_shape + [jax.ShapeDtypeStruct((n, d), BF16),
                               jax.ShapeDtypeStruct((n_heads, d), BF16)],
        scratch_shapes=[pltpu.VMEM((tm_h, d), BF16)],
        compiler_params=pltpu.CompilerParams(
            dimension_semantics=("arbitrary",), vmem_limit_bytes=VMEM_LIMIT),
        name="inproj_head",
    )(x2d, norm_w, w_t, w_t)

    any_spec = pl.BlockSpec(memory_space=pl.ANY)
    return pl.pallas_call(
        functools.partial(_inproj_kernel, row_chunk=128),
        grid=(t // tm - skip, n // tn),
        in_specs=[
            pl.BlockSpec((tm, d), lambda i, j: (i + skip, 0)),
            pl.BlockSpec((1, d), lambda i, j: (0, 0)),
            pl.BlockSpec((tn, d), lambda i, j: (j, 0)),
            pl.BlockSpec((n_heads, d), lambda i, j: (0, 0)),
            any_spec, any_spec,
        ],
        out_specs=[
            pl.BlockSpec((tm, tn), lambda i, j: (i + skip, j)),
            pl.BlockSpec((n_heads, tm), lambda i, j: (0, i + skip)),
        ],
        out_shape=out_shape,
        input_output_aliases={4: 0, 5: 1},
        scratch_shapes=[pltpu.VMEM((tm, d), BF16)],
        compiler_params=pltpu.CompilerParams(
            dimension_semantics=("arbitrary", "arbitrary"),
            vmem_limit_bytes=VMEM_LIMIT),
        name="inproj",
    )(x2d, norm_w, w_bf, w_dt_bf, proj, dt_t)


def _gmlp_rows(u_ref, v_ref, z_ref, lnw_ref, lnb_ref, ws_ref, bst_ref, y_ref):
    tb, e = u_ref.shape
    l = GMLP_CHUNK
    gd = e // GMLP_GROUPS
    row = lax.broadcasted_iota(jnp.int32, (l, l), 0)
    col = lax.broadcasted_iota(jnp.int32, (l, l), 1)
    causal = col <= row
    wm = [jnp.where(causal, ws_ref[g], 0.0).astype(BF16) for g in range(GMLP_GROUPS)]
    lnw = lnw_ref[...]
    lnb = lnb_ref[...]

    for ci in range(tb // l):
        rows = slice(ci * l, (ci + 1) * l)
        v = v_ref[rows, :].astype(F32)
        mu = jnp.mean(v, axis=-1, keepdims=True)
        dv = v - mu
        var = jnp.mean(dv * dv, axis=-1, keepdims=True)
        vn = (dv * lax.rsqrt(var + EPS) * lnw + lnb).astype(BF16)
        for g in range(GMLP_GROUPS):
            cs = slice(g * gd, (g + 1) * gd)
            mixed = jnp.dot(wm[g], vn[:, cs], preferred_element_type=F32) + bst_ref[:, g:g + 1]
            gate = _silu(z_ref[rows, cs]) * u_ref[rows, cs]
            y_ref[rows, cs] = gate * mixed.astype(BF16)


def _ssd_kernel(zx_ref, dtt_ref, dtb_ref, alog_ref, shift_ref, cw_ref, cbias_ref, dskip_ref,
                nw_ref, e3_ref, wb_ref, wo_ref, y_ref, wbbf_ref, wobf_ref, xx_ref, xc_ref, ht_ref,
                we_ref):
    wbbf_ref[...] = wb_ref[...].astype(wbbf_ref.dtype)
    wobf_ref[...] = wo_ref[...].astype(wobf_ref.dtype)

    zb_ref = zx_ref.at[:, 0:y_ref.shape[1]]
    xbc_ref = zx_ref.at[:, y_ref.shape[1]:]
    c = pl.program_id(1)
    l = SSD_CHUNK
    n_heads = dtt_ref.shape[0]
    width = zb_ref.shape[1]
    conv_k = cw_ref.shape[0]
    n = SSD_STATE
    g_n = SSD_GROUPS
    gw = width // g_n
    hpg = gw // SSD_HEAD_DIM
    tail = xx_ref.shape[0] - l

    @pl.when(c == 0)
    def _init():
        xx_ref[0:tail, :] = jnp.zeros((tail, xx_ref.shape[1]), BF16)
        ht_ref[...] = jnp.zeros(ht_ref.shape, F32)

    def chunk(ci, carry):
        rows = pl.ds(pl.multiple_of(ci * l, l), l)
        cur = xbc_ref[rows, :]
        xx_ref[tail:tail + l, :] = cur
        shifted = jnp.dot(shift_ref[...], xx_ref[...], preferred_element_type=F32)
        n_slabs = l // SHIFT_SLAB
        tap_rows = lambda k: jnp.concatenate(
            [shifted[(s * (conv_k - 1) + k) * SHIFT_SLAB:(s * (conv_k - 1) + k + 1) * SHIFT_SLAB, :]
             for s in range(n_slabs)], axis=0)
        acc = cbias_ref[...]
        for k in range(conv_k - 1):
            acc = acc + tap_rows(k) * cw_ref[k:k + 1, :]
        acc = acc + cur.astype(F32) * cw_ref[conv_k - 1:conv_k, :]
        xc_ref[...] = _silu(acc)
        xx_ref[0:tail, :] = cur[l - tail:l, :]

        row = lax.broadcasted_iota(jnp.int32, (l, l), 0)
        col = lax.broadcasted_iota(jnp.int32, (l, l), 1)
        causal = col <= row
        triu = (row <= col).astype(BF16)
        dt_t = _softplus(dtt_ref[:, rows] + dtb_ref[...])
        adt_t = dt_t * (-jnp.exp(alog_ref[...]))
        a_row = None
        for term in _split_bf16(adt_t, SPLIT_TERMS):
            part = jnp.dot(term.astype(BF16), triu, preferred_element_type=F32)
            a_row = part if a_row is None else a_row + part
        a_last = a_row[:, l - 1:l]
        ap = a_row - jnp.log(dt_t)

        zpad = jnp.zeros((LANES - n_heads, l), F32)
        to_col = lambda v: jnp.concatenate([v, zpad], axis=0).T
        a_col = to_col(a_row)

        def pack(v):
            hi, mid, lo = _split_bf16(v, SPLIT_TERMS)
            return (hi + pltpu.roll(mid, n_heads, axis=1)
                    + pltpu.roll(lo, 2 * n_heads, axis=1)).astype(BF16)

        e3 = e3_ref[...]
        we_ref[0:l, :] = jnp.dot(pack(to_col(dt_t * jnp.exp(a_last - a_row))), e3,
                                 preferred_element_type=F32)
        we_ref[l:2 * l, :] = jnp.dot(pack(to_col(jnp.exp(a_row))), e3, preferred_element_type=F32)

        lane_g = lax.broadcasted_iota(jnp.int32, (l, gw), 1)
        for g in range(g_n):
            gs = slice(g * gw, (g + 1) * gw)
            b_g = xc_ref[:, width + g * n: width + (g + 1) * n].astype(BF16)
            c_g = xc_ref[:, width + (g_n + g) * n: width + (g_n + g + 1) * n].astype(BF16)
            cb = lax.dot_general(c_g, b_g, NT_DIMS, preferred_element_type=F32)
            cb = jnp.where(causal, cb, 0.0)
            xs_g = xc_ref[:, gs]
            xs_bf = xs_g.astype(BF16)

            scores = []
            xbd = []
            for k in range(hpg):
                h = g * hpg + k
                seg = a_col[:, h:h + 1] - ap[h:h + 1, :]
                scores.append((cb * jnp.exp(jnp.where(causal, seg, 0.0))).astype(BF16))
                in_head = (lane_g >= k * SSD_HEAD_DIM) & (lane_g < (k + 1) * SSD_HEAD_DIM)
                xbd.append(jnp.where(in_head, xs_bf, jnp.zeros_like(xs_bf)))
            s_cat = jnp.concatenate(scores, axis=1)
            x_bd = jnp.concatenate(xbd, axis=0)
            y_diag = jnp.dot(s_cat, x_bd, preferred_element_type=F32)

            h_prev = ht_ref[g]
            y_off = jnp.dot(c_g, h_prev.astype(BF16), preferred_element_type=F32) * we_ref[l:2 * l, gs]
            x_dec = (xs_g * we_ref[0:l, gs]).astype(BF16)
            st = lax.dot_general(b_g, x_dec, TN_DIMS, preferred_element_type=F32)
            ht_ref[g] = h_prev * we_ref[2 * l - 1:2 * l, gs] + st

            y = y_diag + y_off + dskip_ref[:, gs] * xs_g
            y = y * _silu(zb_ref[rows, gs]).astype(F32)
            ms = jnp.mean(y * y, axis=-1, keepdims=True)
            y_ref[rows, gs] = (y * lax.rsqrt(ms + EPS) * nw_ref[:, gs]).astype(y_ref.dtype)
        return carry

    def chunk_group(gi, carry):
        for k in range(SSD_UNROLL):
            carry = chunk(SSD_UNROLL * gi + k, carry)
        return carry

    lax.fori_loop(0, zb_ref.shape[0] // (SSD_UNROLL * l), chunk_group, 0)


def _ssd(proj, dt_t, dtb_col, alog_col, shift, conv_w, conv_b, dskip_e, norm_w, e3, w_branch2d,
         w_out, batch, seq, width):
    t = proj.shape[0]
    n_heads = dt_t.shape[0]
    l = SSD_CHUNK
    conv_k, conv_dim = conv_w.shape
    assert conv_dim == width + 2 * SSD_GROUPS * SSD_STATE
    assert width % SSD_GROUPS == 0 and conv_k - 1 <= l and SPLIT_TERMS * n_heads <= LANES
    assert shift.shape == ((conv_k - 1) * l, CONV_TAIL + l) and conv_k - 1 <= CONV_TAIL
    zx_w = width + conv_dim
    zx_blk = (3 * width) // zx_w
    assert zx_blk * zx_w == 3 * width
    tb = SSD_TB
    nb = seq // tb
    assert seq % tb == 0 and tb % l == 0
    n_steps = batch * nb
    wb_rows, wo_rows = w_branch2d.shape[0] // n_steps, w_out.shape[0] // n_steps
    assert wb_rows * n_steps == w_branch2d.shape[0] and wo_rows * n_steps == w_out.shape[0]
    assert wb_rows % (2 * SUBLANES) == 0 and wo_rows % (2 * SUBLANES) == 0
    tok = lambda b, c: b * nb + c
    const = lambda b, c: (0, 0)
    slab = lambda rows, arr: pl.BlockSpec((rows, arr.shape[1]), lambda b, c: (tok(b, c), 0))
    return pl.pallas_call(
        _ssd_kernel,
        grid=(batch, nb),
        in_specs=[
            pl.BlockSpec((tb, zx_w), lambda b, c: (tok(b, c), zx_blk)),
            pl.BlockSpec((n_heads, tb), lambda b, c: (0, tok(b, c))),
            pl.BlockSpec((n_heads, 1), const),
            pl.BlockSpec((n_heads, 1), const),
            pl.BlockSpec(shift.shape, const),
            pl.BlockSpec((conv_k, conv_dim), const),
            pl.BlockSpec((1, conv_dim), const),
            pl.BlockSpec((1, width), const),
            pl.BlockSpec((1, width), const),
            pl.BlockSpec((LANES, width), const),
            slab(wb_rows, w_branch2d),
            slab(wo_rows, w_out),
        ],
        out_specs=[
            pl.BlockSpec((tb, width), lambda b, c: (tok(b, c), 0)),
            slab(wb_rows, w_branch2d),
            slab(wo_rows, w_out),
        ],
        out_shape=[
            jax.ShapeDtypeStruct((t, width), BF16),
            jax.ShapeDtypeStruct(w_branch2d.shape, BF16),
            jax.ShapeDtypeStruct(w_out.shape, BF16),
        ],
        scratch_shapes=[
            pltpu.VMEM((CONV_TAIL + l, conv_dim), BF16),
            pltpu.VMEM((l, conv_dim), F32),
            pltpu.VMEM((SSD_GROUPS, SSD_STATE, width // SSD_GROUPS), F32),
            pltpu.VMEM((2 * l, width), F32),
        ],
        compiler_params=pltpu.CompilerParams(
            dimension_semantics=("arbitrary", "arbitrary"), vmem_limit_bytes=VMEM_LIMIT),
        name="ssd",
    )(proj, dt_t, dtb_col, alog_col, shift, conv_w, conv_b, dskip_e, norm_w, e3, w_branch2d, w_out)


def _merge_kernel(uvz_ref, lnw_ref, lnb_ref, ws_ref, bst_ref, yb_ref, gl_ref, x_ref,
                  bg_ref, wb_ref, wo_ref, fnw_ref, o_ref, ya_ref, *, final):
    d = x_ref.shape[1]
    e = yb_ref.shape[1]
    u_ref, v_ref, za_ref = (uvz_ref.at[:, k * e:(k + 1) * e] for k in range(3))
    bd1 = jnp.dot(yb_ref[...], wb_ref[e:2 * e, :], preferred_element_type=F32)
    _gmlp_rows(u_ref, v_ref, za_ref, lnw_ref, lnb_ref, ws_ref, bst_ref, ya_ref)
    bd0 = jnp.dot(ya_ref[...], wb_ref[0:e, :], preferred_element_type=F32)
    g0 = _sigmoid(gl_ref[:, 0:d].astype(F32) + bg_ref[0:1, :])
    g1 = _sigmoid(gl_ref[:, d:2 * d].astype(F32) + bg_ref[1:2, :])
    merged = (g0 * bd0 + g1 * bd1).astype(BF16)
    xn = x_ref[...] + jnp.dot(merged, wo_ref[...], preferred_element_type=F32)
    if final:
        ms = jnp.mean(xn * xn, axis=-1, keepdims=True)
        xn = xn * lax.rsqrt(ms + EPS) * fnw_ref[...]
    o_ref[...] = xn


def _merge(proj, ln_w, ln_b, w_s, b_s_t, y_b, x2d, b_gate, w_branch, w_out, final_norm_w, final):
    t, d = x2d.shape
    e = y_b.shape[1]
    tm = MERGE_TM
    g, l, _ = w_s.shape
    gate_blk = (proj.shape[1] - N_BRANCH * d) // (N_BRANCH * d)
    assert t % tm == 0 and gate_blk * N_BRANCH * d == proj.shape[1] - N_BRANCH * d
    assert l == GMLP_CHUNK and g == GMLP_GROUPS and tm % l == 0
    resident = dict(pipeline_mode=pl.Buffered(1))
    return pl.pallas_call(
        functools.partial(_merge_kernel, final=final),
        grid=(t // tm,),
        in_specs=[
            pl.BlockSpec((tm, 3 * e), lambda i: (i, 0)),
            pl.BlockSpec((1, e), lambda i: (0, 0)),
            pl.BlockSpec((1, e), lambda i: (0, 0)),
            pl.BlockSpec((g, l, l), lambda i: (0, 0, 0)),
            pl.BlockSpec((l, g), lambda i: (0, 0)),
            pl.BlockSpec((tm, e), lambda i: (i, 0)),
            pl.BlockSpec((tm, N_BRANCH * d), lambda i: (i, gate_blk)),
            pl.BlockSpec((tm, d), lambda i: (i, 0)),
            pl.BlockSpec((N_BRANCH, d), lambda i: (0, 0)),
            pl.BlockSpec((N_BRANCH * e, d), lambda i: (0, 0), **resident),
            pl.BlockSpec((d, d), lambda i: (0, 0), **resident),
            pl.BlockSpec((1, d), lambda i: (0, 0)),
        ],
        out_specs=pl.BlockSpec((tm, d), lambda i: (i, 0)),
        out_shape=jax.ShapeDtypeStruct((t, d), F32),
        scratch_shapes=[pltpu.VMEM((tm, e), BF16)],
        compiler_params=pltpu.CompilerParams(
            dimension_semantics=("arbitrary",), vmem_limit_bytes=VMEM_LIMIT),
        name="merge",
    )(proj, ln_w, ln_b, w_s, b_s_t, y_b, proj, x2d, b_gate, w_branch, w_out, final_norm_w)


def kernel(x, norm_w, w_in, b_gate, ln_v_w, ln_v_b, w_spatial, b_spatial, conv_w, conv_b,
           dt_bias, a_log, d_skip, ssm_norm_w, w_branch, w_out, final_norm_w):
    batch, seq, d = x.shape
    depth = norm_w.shape[0]
    width = ln_v_w.shape[1]
    n_heads = dt_bias.shape[1]
    conv_k, conv_dim = conv_w.shape[1:]
    assert width == d and n_heads * SSD_HEAD_DIM == width
    o_dt = 4 * width + conv_dim
    t = batch * seq
    x2d = x.reshape(t, d)

    head_of_col = jnp.arange(width, dtype=jnp.int32) // SSD_HEAD_DIM
    e_rows = jnp.arange(LANES, dtype=jnp.int32)
    e3 = ((e_rows[:, None] % n_heads == head_of_col[None, :])
          & (e_rows[:, None] < SPLIT_TERMS * n_heads)).astype(BF16)
    out_row = jnp.arange((conv_k - 1) * SSD_CHUNK, dtype=jnp.int32)
    slab, in_slab = out_row // ((conv_k - 1) * SHIFT_SLAB), out_row % ((conv_k - 1) * SHIFT_SLAB)
    tap, pos = in_slab // SHIFT_SLAB, slab * SHIFT_SLAB + in_slab % SHIFT_SLAB
    src = CONV_TAIL + pos - (conv_k - 1) + tap
    shift = (src[:, None]
             == jnp.arange(CONV_TAIL + SSD_CHUNK, dtype=jnp.int32)[None, :]).astype(BF16)

    for layer in range(depth):
        proj, dt_t = _inproj(x2d, norm_w[layer][None, :], jnp.swapaxes(w_in[layer], 0, 1), o_dt,
                             n_heads)

        y_b, w_branch_bf, w_out_bf = _ssd(
            proj, dt_t, dt_bias[layer][:, None], a_log[layer][:, None], shift, conv_w[layer],
            conv_b[layer][None, :], jnp.repeat(d_skip[layer], SSD_HEAD_DIM)[None, :],
            ssm_norm_w[layer][None, :], e3, w_branch[layer].reshape(N_BRANCH * width, d),
            w_out[layer], batch, seq, width)

        x2d = _merge(proj, ln_v_w[layer][None, :], ln_v_b[layer][None, :], w_spatial[layer],
                     b_spatial[layer].T, y_b, x2d, b_gate[layer], w_branch_bf, w_out_bf,
                     final_norm_w[None, :], final=(layer == depth - 1))
    return x2d.reshape(batch, seq, d)
```

```python
import functools

import jax
import jax.numpy as jnp
from jax import lax
from jax.experimental import pallas as pl
from jax.experimental.pallas import tpu as pltpu

F32 = jnp.float32
BF16 = jnp.bfloat16

EPS = 1e-5
GMLP_GROUPS = 8
GMLP_CHUNK = 128
SSD_HEAD_DIM = 64
SSD_GROUPS = 8
SSD_STATE = 128
SSD_CHUNK = 128
N_BRANCH = 2

LANES = 128
SUBLANES = 8
SPLIT_TERMS = 3
CONV_TAIL = 2 * SUBLANES
SHIFT_SLAB = 2 * SUBLANES

INPROJ_HEAD_TM = 2048
INPROJ_HEAD_TN = 512
INPROJ_TM = 1024
INPROJ_TN = 2048
SSD_TB = 512
SSD_UNROLL = 4
MERGE_TM = 256
VMEM_LIMIT = 60000 * 1024

NT_DIMS = (((1,), (1,)), ((), ()))
TN_DIMS = (((0,), (0,)), ((), ()))


def _silu(x):
    hx = 0.5 * x
    return hx + hx * jnp.tanh(hx)


def _sigmoid(x):
    return 1.0 / (1.0 + jnp.exp(-x))


def _softplus(x):
    return jnp.maximum(x, 0.0) + jnp.log1p(jnp.exp(-jnp.abs(x)))


def _split_bf16(v, n_terms):
    terms = []
    r = v
    for _ in range(n_terms):
        t = r.astype(BF16).astype(F32)
        terms.append(t)
        r = r - t
    return terms


def _rmsnorm_rows(x_ref, nw_ref, h_ref, row_chunk):
    def body(r, carry):
        r0 = pl.multiple_of(r * row_chunk, row_chunk)
        x = x_ref[pl.ds(r0, row_chunk), :]
        ms = jnp.mean(x * x, axis=-1, keepdims=True)
        h = x * lax.rsqrt(ms + EPS) * nw_ref[...]
        h_ref[pl.ds(r0, row_chunk), :] = h.astype(BF16)
        return carry
    lax.fori_loop(0, x_ref.shape[0] // row_chunk, body, 0)


def _inproj_head_kernel(x_ref, nw_ref, w_ref, wdt_ref, proj_ref, dtt_ref, wbf_ref, wdtbf_ref,
                        h_ref, *, row_chunk):
    @pl.when(pl.program_id(0) == 0)
    def _norm():
        _rmsnorm_rows(x_ref, nw_ref, h_ref, row_chunk)
        wdt = wdt_ref[...].astype(BF16)
        wdtbf_ref[...] = wdt
        dtt_ref[...] = lax.dot_general(wdt, h_ref[...], NT_DIMS, preferred_element_type=F32)

    w = w_ref[...].astype(BF16)
    wbf_ref[...] = w
    proj_ref[...] = lax.dot_general(h_ref[...], w, NT_DIMS,
                                    preferred_element_type=F32).astype(proj_ref.dtype)


def _inproj_kernel(x_ref, nw_ref, w_ref, wdt_ref, proj_in_ref, dtt_in_ref, proj_ref, dtt_ref,
                   h_ref, *, row_chunk):
    del proj_in_ref, dtt_in_ref
    j = pl.program_id(1)

    def project():
        proj_ref[...] = lax.dot_general(h_ref[...], w_ref[...], NT_DIMS,
                                        preferred_element_type=F32).astype(proj_ref.dtype)

    @pl.when(j == 0)
    def _first_tile():
        for r in range(x_ref.shape[0] // row_chunk):
            rows = slice(r * row_chunk, (r + 1) * row_chunk)
            x = x_ref[rows, :]
            ms = jnp.mean(x * x, axis=-1, keepdims=True)
            h_ref[rows, :] = (x * lax.rsqrt(ms + EPS) * nw_ref[...]).astype(BF16)
        dtt_ref[...] = lax.dot_general(wdt_ref[...], h_ref[...], NT_DIMS,
                                       preferred_element_type=F32)
        project()

    pl.when(j != 0)(project)


def _inproj(x2d, norm_w, w_t, o_dt, n_heads):
    t, d = x2d.shape
    n = w_t.shape[0] - n_heads
    tm, tn, tm_h, tn_h = INPROJ_TM, INPROJ_TN, INPROJ_HEAD_TM, INPROJ_HEAD_TN
    skip = tm_h // tm
    assert t % tm == 0 and n % tn == 0 and n % tn_h == 0 and tm_h % tm == 0 and t > tm_h
    assert o_dt % tn_h == 0 and o_dt % n_heads == 0
    out_shape = [jax.ShapeDtypeStruct((t, n), BF16), jax.ShapeDtypeStruct((n_heads, t), F32)]
    w_row = lambda j: (
        pl.multiple_of(j * tn_h + jnp.where(j * tn_h >= o_dt, n_heads, 0), n_heads), 0)
    proj, dt_t, w_bf, w_dt_bf = pl.pallas_call(
        functools.partial(_inproj_head_kernel, row_chunk=128),
        grid=(n // tn_h,),
        in_specs=[
            pl.BlockSpec((tm_h, d), lambda j: (0, 0), pipeline_mode=pl.Buffered(1)),
            pl.BlockSpec((1, d), lambda j: (0, 0)),
            pl.BlockSpec((pl.Element(tn_h), pl.Element(d)), w_row),
            pl.BlockSpec((n_heads, d), lambda j: (o_dt // n_heads, 0)),
        ],
        out_specs=[
            pl.BlockSpec((tm_h, tn_h), lambda j: (0, j)),
            pl.BlockSpec((n_heads, tm_h), lambda j: (0, 0)),
            pl.BlockSpec((tn_h, d), lambda j: (j, 0)),
            pl.BlockSpec((n_heads, d), lambda j: (0, 0)),
        ],
        out_shape=out_shape + [jax.ShapeDtypeStruct((n, d), BF16),
                               jax.ShapeDtypeStruct((n_heads, d), BF16)],
        scratch_shapes=[pltpu.VMEM((tm_h, d), BF16)],
        compiler_params=pltpu.CompilerParams(
            dimension_semantics=("arbitrary",), vmem_limit_bytes=VMEM_LIMIT),
        name="inproj_head",
    )(x2d, norm_w, w_t, w_t)

    any_spec = pl.BlockSpec(memory_space=pl.ANY)
    return pl.pallas_call(
        functools.partial(_inproj_kernel, row_chunk=128),
        grid=(t // tm - skip, n // tn),
        in_specs=[
            pl.BlockSpec((tm, d), lambda i, j: (i + skip, 0)),
            pl.BlockSpec((1, d), lambda i, j: (0, 0)),
            pl.BlockSpec((tn, d), lambda i, j: (j, 0)),
            pl.BlockSpec((n_heads, d), lambda i, j: (0, 0)),
            any_spec, any_spec,
        ],
        out_specs=[
            pl.BlockSpec((tm, tn), lambda i, j: (i + skip, j)),
            pl.BlockSpec((n_heads, tm), lambda i, j: (0, i + skip)),
        ],
        out_shape=out_shape,
        input_output_aliases={4: 0, 5: 1},
        scratch_shapes=[pltpu.VMEM((tm, d), BF16)],
        compiler_params=pltpu.CompilerParams(
            dimension_semantics=("arbitrary", "arbitrary"),
            vmem_limit_bytes=VMEM_LIMIT),
        name="inproj",
    )(x2d, norm_w, w_bf, w_dt_bf, proj, dt_t)


def _gmlp_rows(u_ref, v_ref, z_ref, lnw_ref, lnb_ref, ws_ref, bst_ref, y_ref):
    tb, e = u_ref.shape
    l = GMLP_CHUNK
    gd = e // GMLP_GROUPS
    row = lax.broadcasted_iota(jnp.int32, (l, l), 0)
    col = lax.broadcasted_iota(jnp.int32, (l, l), 1)
    causal = col <= row
    wm = [jnp.where(causal, ws_ref[g], 0.0).astype(BF16) for g in range(GMLP_GROUPS)]
    lnw = lnw_ref[...]
    lnb = lnb_ref[...]

    for ci in range(tb // l):
        rows = slice(ci * l, (ci + 1) * l)
        v = v_ref[rows, :].astype(F32)
        mu = jnp.mean(v, axis=-1, keepdims=True)
        dv = v - mu
        var = jnp.mean(dv * dv, axis=-1, keepdims=True)
        vn = (dv * lax.rsqrt(var + EPS) * lnw + lnb).astype(BF16)
        for g in range(GMLP_GROUPS):
            cs = slice(g * gd, (g + 1) * gd)
            mixed = jnp.dot(wm[g], vn[:, cs], preferred_element_type=F32) + bst_ref[:, g:g + 1]
            gate = _silu(z_ref[rows, cs]) * u_ref[rows, cs]
            y_ref[rows, cs] = gate * mixed.astype(BF16)


def _ssd_kernel(zx_ref, dtt_ref, dtb_ref, alog_ref, shift_ref, cw_ref, cbias_ref, dskip_ref,
                nw_ref, e3_ref, wb_ref, wo_ref, y_ref, wbbf_ref, wobf_ref, xx_ref, xc_ref, ht_ref,
                we_ref):
    wbbf_ref[...] = wb_ref[...].astype(wbbf_ref.dtype)
    wobf_ref[...] = wo_ref[...].astype(wobf_ref.dtype)

    zb_ref = zx_ref.at[:, 0:y_ref.shape[1]]
    xbc_ref = zx_ref.at[:, y_ref.shape[1]:]
    c = pl.program_id(1)
    l = SSD_CHUNK
    n_heads = dtt_ref.shape[0]
    width = zb_ref.shape[1]
    conv_k = cw_ref.shape[0]
    n = SSD_STATE
    g_n = SSD_GROUPS
    gw = width // g_n
    hpg = gw // SSD_HEAD_DIM
    tail = xx_ref.shape[0] - l

    @pl.when(c == 0)
    def _init():
        xx_ref[0:tail, :] = jnp.zeros((tail, xx_ref.shape[1]), BF16)
        ht_ref[...] = jnp.zeros(ht_ref.shape, F32)

    def chunk(ci, carry):
        rows = pl.ds(pl.multiple_of(ci * l, l), l)
        cur = xbc_ref[rows, :]
        xx_ref[tail:tail + l, :] = cur
        shifted = jnp.dot(shift_ref[...], xx_ref[...], preferred_element_type=F32)
        n_slabs = l // SHIFT_SLAB
        tap_rows = lambda k: jnp.concatenate(
            [shifted[(s * (conv_k - 1) + k) * SHIFT_SLAB:(s * (conv_k - 1) + k + 1) * SHIFT_SLAB, :]
             for s in range(n_slabs)], axis=0)
        acc = cbias_ref[...]
        for k in range(conv_k - 1):
            acc = acc + tap_rows(k) * cw_ref[k:k + 1, :]
        acc = acc + cur.astype(F32) * cw_ref[conv_k - 1:conv_k, :]
        xc_ref[...] = _silu(acc)
        xx_ref[0:tail, :] = cur[l - tail:l, :]

        row = lax.broadcasted_iota(jnp.int32, (l, l), 0)
        col = lax.broadcasted_iota(jnp.int32, (l, l), 1)
        causal = col <= row
        triu = (row <= col).astype(BF16)
        dt_t = _softplus(dtt_ref[:, rows] + dtb_ref[...])
        adt_t = dt_t * (-jnp.exp(alog_ref[...]))
        a_row = None
        for term in _split_bf16(adt_t, SPLIT_TERMS):
            part = jnp.dot(term.astype(BF16), triu, preferred_element_type=F32)
            a_row = part if a_row is None else a_row + part
        a_last = a_row[:, l - 1:l]
        ap = a_row - jnp.log(dt_t)

        zpad = jnp.zeros((LANES - n_heads, l), F32)
        to_col = lambda v: jnp.concatenate([v, zpad], axis=0).T
        a_col = to_col(a_row)

        def pack(v):
            hi, mid, lo = _split_bf16(v, SPLIT_TERMS)
            return (hi + pltpu.roll(mid, n_heads, axis=1)
                    + pltpu.roll(lo, 2 * n_heads, axis=1)).astype(BF16)

        e3 = e3_ref[...]
        we_ref[0:l, :] = jnp.dot(pack(to_col(dt_t * jnp.exp(a_last - a_row))), e3,
                                 preferred_element_type=F32)
        we_ref[l:2 * l, :] = jnp.dot(pack(to_col(jnp.exp(a_row))), e3, preferred_element_type=F32)

        lane_g = lax.broadcasted_iota(jnp.int32, (l, gw), 1)
        for g in range(g_n):
            gs = slice(g * gw, (g + 1) * gw)
            b_g = xc_ref[:, width + g * n: width + (g + 1) * n].astype(BF16)
            c_g = xc_ref[:, width + (g_n + g) * n: width + (g_n + g + 1) * n].astype(BF16)
            cb = lax.dot_general(c_g, b_g, NT_DIMS, preferred_element_type=F32)
            cb = jnp.where(causal, cb, 0.0)
            xs_g = xc_ref[:, gs]
            xs_bf = xs_g.astype(BF16)

            scores = []
            xbd = []
            for k in range(hpg):
                h = g * hpg + k
                seg = a_col[:, h:h + 1] - ap[h:h + 1, :]
                scores.append((cb * jnp.exp(jnp.where(causal, seg, 0.0))).astype(BF16))
                in_head = (lane_g >= k * SSD_HEAD_DIM) & (lane_g < (k + 1) * SSD_HEAD_DIM)
                xbd.append(jnp.where(in_head, xs_bf, jnp.zeros_like(xs_bf)))
            s_cat = jnp.concatenate(scores, axis=1)
            x_bd = jnp.concatenate(xbd, axis=0)
            y_diag = jnp.dot(s_cat, x_bd, preferred_element_type=F32)

            h_prev = ht_ref[g]
            y_off = jnp.dot(c_g, h_prev.astype(BF16), preferred_element_type=F32) * we_ref[l:2 * l, gs]
            x_dec = (xs_g * we_ref[0:l, gs]).astype(BF16)
            st = lax.dot_general(b_g, x_dec, TN_DIMS, preferred_element_type=F32)
            ht_ref[g] = h_prev * we_ref[2 * l - 1:2 * l, gs] + st

            y = y_diag + y_off + dskip_ref[:, gs] * xs_g
            y = y * _silu(zb_ref[rows, gs]).astype(F32)
            ms = jnp.mean(y * y, axis=-1, keepdims=True)
            y_ref[rows, gs] = (y * lax.rsqrt(ms + EPS) * nw_ref[:, gs]).astype(y_ref.dtype)
        return carry

    def chunk_group(gi, carry):
        for k in range(SSD_UNROLL):
            carry = chunk(SSD_UNROLL * gi + k, carry)
        return carry

    lax.fori_loop(0, zb_ref.shape[0] // (SSD_UNROLL * l), chunk_group, 0)


def _ssd(proj, dt_t, dtb_col, alog_col, shift, conv_w, conv_b, dskip_e, norm_w, e3, w_branch2d,
         w_out, batch, seq, width):
    t = proj.shape[0]
    n_heads = dt_t.shape[0]
    l = SSD_CHUNK
    conv_k, conv_dim = conv_w.shape
    assert conv_dim == width + 2 * SSD_GROUPS * SSD_STATE
    assert width % SSD_GROUPS == 0 and conv_k - 1 <= l and SPLIT_TERMS * n_heads <= LANES
    assert shift.shape == ((conv_k - 1) * l, CONV_TAIL + l) and conv_k - 1 <= CONV_TAIL
    zx_w = width + conv_dim
    zx_blk = (3 * width) // zx_w
    assert zx_blk * zx_w == 3 * width
    tb = SSD_TB
    nb = seq // tb
    assert seq % tb == 0 and tb % l == 0
    n_steps = batch * nb
    wb_rows, wo_rows = w_branch2d.shape[0] // n_steps, w_out.shape[0] // n_steps
    assert wb_rows * n_steps == w_branch2d.shape[0] and wo_rows * n_steps == w_out.shape[0]
    assert wb_rows % (2 * SUBLANES) == 0 and wo_rows % (2 * SUBLANES) == 0
    tok = lambda b, c: b * nb + c
    const = lambda b, c: (0, 0)
    slab = lambda rows, arr: pl.BlockSpec((rows, arr.shape[1]), lambda b, c: (tok(b, c), 0))
    return pl.pallas_call(
        _ssd_kernel,
        grid=(batch, nb),
        in_specs=[
            pl.BlockSpec((tb, zx_w), lambda b, c: (tok(b, c), zx_blk)),
            pl.BlockSpec((n_heads, tb), lambda b, c: (0, tok(b, c))),
            pl.BlockSpec((n_heads, 1), const),
            pl.BlockSpec((n_heads, 1), const),
            pl.BlockSpec(shift.shape, const),
            pl.BlockSpec((conv_k, conv_dim), const),
            pl.BlockSpec((1, conv_dim), const),
            pl.BlockSpec((1, width), const),
            pl.BlockSpec((1, width), const),
            pl.BlockSpec((LANES, width), const),
            slab(wb_rows, w_branch2d),
            slab(wo_rows, w_out),
        ],
        out_specs=[
            pl.BlockSpec((tb, width), lambda b, c: (tok(b, c), 0)),
            slab(wb_rows, w_branch2d),
            slab(wo_rows, w_out),
        ],
        out_shape=[
            jax.ShapeDtypeStruct((t, width), BF16),
            jax.ShapeDtypeStruct(w_branch2d.shape, BF16),
            jax.ShapeDtypeStruct(w_out.shape, BF16),
        ],
        scratch_shapes=[
            pltpu.VMEM((CONV_TAIL + l, conv_dim), BF16),
            pltpu.VMEM((l, conv_dim), F32),
            pltpu.VMEM((SSD_GROUPS, SSD_STATE, width // SSD_GROUPS), F32),
            pltpu.VMEM((2 * l, width), F32),
        ],
        compiler_params=pltpu.CompilerParams(
            dimension_semantics=("arbitrary", "arbitrary"), vmem_limit_bytes=VMEM_LIMIT),
        name="ssd",
    )(proj, dt_t, dtb_col, alog_col, shift, conv_w, conv_b, dskip_e, norm_w, e3, w_branch2d, w_out)


def _merge_kernel(uvz_ref, lnw_ref, lnb_ref, ws_ref, bst_ref, yb_ref, gl_ref, x_ref,
                  bg_ref, wb_ref, wo_ref, fnw_ref, o_ref, ya_ref, *, final):
    d = x_ref.shape[1]
    e = yb_ref.shape[1]
    u_ref, v_ref, za_ref = (uvz_ref.at[:, k * e:(k + 1) * e] for k in range(3))
    bd1 = jnp.dot(yb_ref[...], wb_ref[e:2 * e, :], preferred_element_type=F32)
    _gmlp_rows(u_ref, v_ref, za_ref, lnw_ref, lnb_ref, ws_ref, bst_ref, ya_ref)
    bd0 = jnp.dot(ya_ref[...], wb_ref[0:e, :], preferred_element_type=F32)
    g0 = _sigmoid(gl_ref[:, 0:d].astype(F32) + bg_ref[0:1, :])
    g1 = _sigmoid(gl_ref[:, d:2 * d].astype(F32) + bg_ref[1:2, :])
    merged = (g0 * bd0 + g1 * bd1).astype(BF16)
    xn = x_ref[...] + jnp.dot(merged, wo_ref[...], preferred_element_type=F32)
    if final:
        ms = jnp.mean(xn * xn, axis=-1, keepdims=True)
        xn = xn * lax.rsqrt(ms + EPS) * fnw_ref[...]
    o_ref[...] = xn


def _merge(proj, ln_w, ln_b, w_s, b_s_t, y_b, x2d, b_gate, w_branch, w_out, final_norm_w, final):
    t, d = x2d.shape
    e = y_b.shape[1]
    tm = MERGE_TM
    g, l, _ = w_s.shape
    gate_blk = (proj.shape[1] - N_BRANCH * d) // (N_BRANCH * d)
    assert t % tm == 0 and gate_blk * N_BRANCH * d == proj.shape[1] - N_BRANCH * d
    assert l == GMLP_CHUNK and g == GMLP_GROUPS and tm % l == 0
    resident = dict(pipeline_mode=pl.Buffered(1))
    return pl.pallas_call(
        functools.partial(_merge_kernel, final=final),
        grid=(t // tm,),
        in_specs=[
            pl.BlockSpec((tm, 3 * e), lambda i: (i, 0)),
            pl.BlockSpec((1, e), lambda i: (0, 0)),
            pl.BlockSpec((1, e), lambda i: (0, 0)),
            pl.BlockSpec((g, l, l), lambda i: (0, 0, 0)),
            pl.BlockSpec((l, g), lambda i: (0, 0)),
            pl.BlockSpec((tm, e), lambda i: (i, 0)),
            pl.BlockSpec((tm, N_BRANCH * d), lambda i: (i, gate_blk)),
            pl.BlockSpec((tm, d), lambda i: (i, 0)),
            pl.BlockSpec((N_BRANCH, d), lambda i: (0, 0)),
            pl.BlockSpec((N_BRANCH * e, d), lambda i: (0, 0), **resident),
            pl.BlockSpec((d, d), lambda i: (0, 0), **resident),
            pl.BlockSpec((1, d), lambda i: (0, 0)),
        ],
        out_specs=pl.BlockSpec((tm, d), lambda i: (i, 0)),
        out_shape=jax.ShapeDtypeStruct((t, d), F32),
        scratch_shapes=[pltpu.VMEM((tm, e), BF16)],
        compiler_params=pltpu.CompilerParams(
            dimension_semantics=("arbitrary",), vmem_limit_bytes=VMEM_LIMIT),
        name="merge",
    )(proj, ln_w, ln_b, w_s, b_s_t, y_b, proj, x2d, b_gate, w_branch, w_out, final_norm_w)


def kernel(x, norm_w, w_in, b_gate, ln_v_w, ln_v_b, w_spatial, b_spatial, conv_w, conv_b,
           dt_bias, a_log, d_skip, ssm_norm_w, w_branch, w_out, final_norm_w):
    batch, seq, d = x.shape
    depth = norm_w.shape[0]
    width = ln_v_w.shape[1]
    n_heads = dt_bias.shape[1]
    conv_k, conv_dim = conv_w.shape[1:]
    assert width == d and n_heads * SSD_HEAD_DIM == width
    o_dt = 4 * width + conv_dim
    t = batch * seq
    x2d = x.reshape(t, d)

    head_of_col = jnp.arange(width, dtype=jnp.int32) // SSD_HEAD_DIM
    e_rows = jnp.arange(LANES, dtype=jnp.int32)
    e3 = ((e_rows[:, None] % n_heads == head_of_col[None, :])
          & (e_rows[:, None] < SPLIT_TERMS * n_heads)).astype(BF16)
    out_row = jnp.arange((conv_k - 1) * SSD_CHUNK, dtype=jnp.int32)
    slab, in_slab = out_row // ((conv_k - 1) * SHIFT_SLAB), out_row % ((conv_k - 1) * SHIFT_SLAB)
    tap, pos = in_slab // SHIFT_SLAB, slab * SHIFT_SLAB + in_slab % SHIFT_SLAB
    src = CONV_TAIL + pos - (conv_k - 1) + tap
    shift = (src[:, None]
             == jnp.arange(CONV_TAIL + SSD_CHUNK, dtype=jnp.int32)[None, :]).astype(BF16)

    for layer in range(depth):
        proj, dt_t = _inproj(x2d, norm_w[layer][None, :], jnp.swapaxes(w_in[layer], 0, 1), o_dt,
                             n_heads)

        y_b, w_branch_bf, w_out_bf = _ssd(
            proj, dt_t, dt_bias[layer][:, None], a_log[layer][:, None], shift, conv_w[layer],
            conv_b[layer][None, :], jnp.repeat(d_skip[layer], SSD_HEAD_DIM)[None, :],
            ssm_norm_w[layer][None, :], e3, w_branch[layer].reshape(N_BRANCH * width, d),
            w_out[layer], batch, seq, width)

        x2d = _merge(proj, ln_v_w[layer][None, :], ln_v_b[layer][None, :], w_spatial[layer],
                     b_spatial[layer].T, y_b, x2d, b_gate[layer], w_branch_bf, w_out_bf,
                     final_norm_w[None, :], final=(layer == depth - 1))
    return x2d.reshape(batch, seq, d)
```

```python
import functools

import jax
import jax.numpy as jnp
from jax import lax
from jax.experimental import pallas as pl
from jax.experimental.pallas import tpu as pltpu

F32 = jnp.float32
BF16 = jnp.bfloat16

EPS = 1e-5
GMLP_GROUPS = 8
GMLP_CHUNK = 128
SSD_HEAD_DIM = 64
SSD_GROUPS = 8
SSD_STATE = 128
SSD_CHUNK = 128
N_BRANCH = 2

LANES = 128
SUBLANES = 8
SPLIT_TERMS = 3
CONV_TAIL = 2 * SUBLANES
SHIFT_SLAB = 2 * SUBLANES

INPROJ_HEAD_TM = 2048
INPROJ_HEAD_TN = 512
INPROJ_TM = 1024
INPROJ_TN = 2048
SSD_TB = 512
SSD_UNROLL = 4
MERGE_TM = 256
VMEM_LIMIT = 60000 * 1024

NT_DIMS = (((1,), (1,)), ((), ()))
TN_DIMS = (((0,), (0,)), ((), ()))


def _silu(x):
    hx = 0.5 * x
    return hx + hx * jnp.tanh(hx)


def _sigmoid(x):
    return 0.5 + 0.5 * jnp.tanh(0.5 * x)


def _softplus(x):
    return jnp.maximum(x, 0.0) + jnp.log1p(jnp.exp(-jnp.abs(x)))


def _split_bf16(v, n_terms):
    terms = []
    r = v
    for _ in range(n_terms):
        t = r.astype(BF16).astype(F32)
        terms.append(t)
        r = r - t
    return terms


def _rmsnorm_rows(x_ref, nw_ref, h_ref, row_chunk):
    def body(r, carry):
        r0 = pl.multiple_of(r * row_chunk, row_chunk)
        x = x_ref[pl.ds(r0, row_chunk), :]
        ms = jnp.mean(x * x, axis=-1, keepdims=True)
        h = x * lax.rsqrt(ms + EPS) * nw_ref[...]
        h_ref[pl.ds(r0, row_chunk), :] = h.astype(BF16)
        return carry
    lax.fori_loop(0, x_ref.shape[0] // row_chunk, body, 0)


def _inproj_head_kernel(x_ref, nw_ref, w_ref, wdt_ref, proj_ref, dtt_ref, wbf_ref, wdtbf_ref,
                        h_ref, *, row_chunk):
    @pl.when(pl.program_id(0) == 0)
    def _norm():
        _rmsnorm_rows(x_ref, nw_ref, h_ref, row_chunk)
        wdt = wdt_ref[...].astype(BF16)
        wdtbf_ref[...] = wdt
        dtt_ref[...] = lax.dot_general(wdt, h_ref[...], NT_DIMS, preferred_element_type=F32)

    w = w_ref[...].astype(BF16)
    wbf_ref[...] = w
    proj_ref[...] = lax.dot_general(h_ref[...], w, NT_DIMS,
                                    preferred_element_type=F32).astype(proj_ref.dtype)


def _inproj_kernel(x_ref, nw_ref, w_ref, wdt_ref, proj_in_ref, dtt_in_ref, proj_ref, dtt_ref,
                   h_ref, *, row_chunk):
    del proj_in_ref, dtt_in_ref
    j = pl.program_id(1)

    def project():
        proj_ref[...] = lax.dot_general(h_ref[...], w_ref[...], NT_DIMS,
                                        preferred_element_type=F32).astype(proj_ref.dtype)

    @pl.when(j == 0)
    def _first_tile():
        for r in range(x_ref.shape[0] // row_chunk):
            rows = slice(r * row_chunk, (r + 1) * row_chunk)
            x = x_ref[rows, :]
            ms = jnp.mean(x * x, axis=-1, keepdims=True)
            h_ref[rows, :] = (x * lax.rsqrt(ms + EPS) * nw_ref[...]).astype(BF16)
        dtt_ref[...] = lax.dot_general(wdt_ref[...], h_ref[...], NT_DIMS,
                                       preferred_element_type=F32)
        project()

    pl.when(j != 0)(project)


def _inproj(x2d, norm_w, w_t, o_dt, n_heads):
    t, d = x2d.shape
    n = w_t.shape[0] - n_heads
    tm, tn, tm_h, tn_h = INPROJ_TM, INPROJ_TN, INPROJ_HEAD_TM, INPROJ_HEAD_TN
    skip = tm_h // tm
    assert t % tm == 0 and n % tn == 0 and n % tn_h == 0 and tm_h % tm == 0 and t > tm_h
    assert o_dt % tn_h == 0 and o_dt % n_heads == 0
    out_shape = [jax.ShapeDtypeStruct((t, n), BF16), jax.ShapeDtypeStruct((n_heads, t), F32)]
    w_row = lambda j: (
        pl.multiple_of(j * tn_h + jnp.where(j * tn_h >= o_dt, n_heads, 0), n_heads), 0)
    proj, dt_t, w_bf, w_dt_bf = pl.pallas_call(
        functools.partial(_inproj_head_kernel, row_chunk=128),
        grid=(n // tn_h,),
        in_specs=[
            pl.BlockSpec((tm_h, d), lambda j: (0, 0), pipeline_mode=pl.Buffered(1)),
            pl.BlockSpec((1, d), lambda j: (0, 0)),
            pl.BlockSpec((pl.Element(tn_h), pl.Element(d)), w_row),
            pl.BlockSpec((n_heads, d), lambda j: (o_dt // n_heads, 0)),
        ],
        out_specs=[
            pl.BlockSpec((tm_h, tn_h), lambda j: (0, j)),
            pl.BlockSpec((n_heads, tm_h), lambda j: (0, 0)),
            pl.BlockSpec((tn_h, d), lambda j: (j, 0)),
            pl.BlockSpec((n_heads, d), lambda j: (0, 0)),
        ],
        out_shape=out_shape + [jax.ShapeDtypeStruct((n, d), BF16),
                               jax.ShapeDtypeStruct((n_heads, d), BF16)],
        scratch_shapes=[pltpu.VMEM((tm_h, d), BF16)],
        compiler_params=pltpu.CompilerParams(
            dimension_semantics=("arbitrary",), vmem_limit_bytes=VMEM_LIMIT),
        name="inproj_head",
    )(x2d, norm_w, w_t, w_t)

    any_spec = pl.BlockSpec(memory_space=pl.ANY)
    return pl.pallas_call(
        functools.partial(_inproj_kernel, row_chunk=128),
        grid=(t // tm - skip, n // tn),
        in_specs=[
            pl.BlockSpec((tm, d), lambda i, j: (i + skip, 0)),
            pl.BlockSpec((1, d), lambda i, j: (0, 0)),
            pl.BlockSpec((tn, d), lambda i, j: (j, 0)),
            pl.BlockSpec((n_heads, d), lambda i, j: (0, 0)),
            any_spec, any_spec,
        ],
        out_specs=[
            pl.BlockSpec((tm, tn), lambda i, j: (i + skip, j)),
            pl.BlockSpec((n_heads, tm), lambda i, j: (0, i + skip)),
        ],
        out_shape=out_shape,
        input_output_aliases={4: 0, 5: 1},
        scratch_shapes=[pltpu.VMEM((tm, d), BF16)],
        compiler_params=pltpu.CompilerParams(
            dimension_semantics=("arbitrary", "arbitrary"),
            vmem_limit_bytes=VMEM_LIMIT),
        name="inproj",
    )(x2d, norm_w, w_bf, w_dt_bf, proj, dt_t)


def _gmlp_rows(u_ref, v_ref, z_ref, lnw_ref, lnb_ref, ws_ref, bst_ref, y_ref):
    tb, e = u_ref.shape
    l = GMLP_CHUNK
    gd = e // GMLP_GROUPS
    row = lax.broadcasted_iota(jnp.int32, (l, l), 0)
    col = lax.broadcasted_iota(jnp.int32, (l, l), 1)
    causal = col <= row
    wm = [jnp.where(causal, ws_ref[g], 0.0).astype(BF16) for g in range(GMLP_GROUPS)]
    lnw = lnw_ref[...]
    lnb = lnb_ref[...]

    for ci in range(tb // l):
        rows = slice(ci * l, (ci + 1) * l)
        v = v_ref[rows, :].astype(F32)
        mu = jnp.mean(v, axis=-1, keepdims=True)
        dv = v - mu
        var = jnp.mean(dv * dv, axis=-1, keepdims=True)
        vn = (dv * lax.rsqrt(var + EPS) * lnw + lnb).astype(BF16)
        for g in range(GMLP_GROUPS):
            cs = slice(g * gd, (g + 1) * gd)
            mixed = jnp.dot(wm[g], vn[:, cs], preferred_element_type=F32) + bst_ref[:, g:g + 1]
            gate = _silu(z_ref[rows, cs]) * u_ref[rows, cs]
            y_ref[rows, cs] = gate * mixed.astype(BF16)


def _ssd_kernel(zx_ref, dtt_ref, dtb_ref, alog_ref, shift_ref, cw_ref, cbias_ref, dskip_ref,
                nw_ref, e3_ref, wb_ref, wo_ref, y_ref, wbbf_ref, wobf_ref, xx_ref, xc_ref, ht_ref,
                we_ref):
    wbbf_ref[...] = wb_ref[...].astype(wbbf_ref.dtype)
    wobf_ref[...] = wo_ref[...].astype(wobf_ref.dtype)

    zb_ref = zx_ref.at[:, 0:y_ref.shape[1]]
    xbc_ref = zx_ref.at[:, y_ref.shape[1]:]
    c = pl.program_id(1)
    l = SSD_CHUNK
    n_heads = dtt_ref.shape[0]
    width = zb_ref.shape[1]
    conv_k = cw_ref.shape[0]
    n = SSD_STATE
    g_n = SSD_GROUPS
    gw = width // g_n
    hpg = gw // SSD_HEAD_DIM
    tail = xx_ref.shape[0] - l

    @pl.when(c == 0)
    def _init():
        xx_ref[0:tail, :] = jnp.zeros((tail, xx_ref.shape[1]), BF16)
        ht_ref[...] = jnp.zeros(ht_ref.shape, F32)

    def chunk(ci, carry):
        rows = pl.ds(pl.multiple_of(ci * l, l), l)
        cur = xbc_ref[rows, :]
        xx_ref[tail:tail + l, :] = cur
        shifted = jnp.dot(shift_ref[...], xx_ref[...], preferred_element_type=F32)
        n_slabs = l // SHIFT_SLAB
        tap_rows = lambda k: jnp.concatenate(
            [shifted[(s * (conv_k - 1) + k) * SHIFT_SLAB:(s * (conv_k - 1) + k + 1) * SHIFT_SLAB, :]
             for s in range(n_slabs)], axis=0)
        acc = cbias_ref[...]
        for k in range(conv_k - 1):
            acc = acc + tap_rows(k) * cw_ref[k:k + 1, :]
        acc = acc + cur.astype(F32) * cw_ref[conv_k - 1:conv_k, :]
        xc_ref[...] = _silu(acc)
        xx_ref[0:tail, :] = cur[l - tail:l, :]

        row = lax.broadcasted_iota(jnp.int32, (l, l), 0)
        col = lax.broadcasted_iota(jnp.int32, (l, l), 1)
        causal = col <= row
        triu = (row <= col).astype(BF16)
        dt_t = _softplus(dtt_ref[:, rows] + dtb_ref[...])
        adt_t = dt_t * (-jnp.exp(alog_ref[...]))
        a_row = None
        for term in _split_bf16(adt_t, SPLIT_TERMS):
            part = jnp.dot(term.astype(BF16), triu, preferred_element_type=F32)
            a_row = part if a_row is None else a_row + part
        a_last = a_row[:, l - 1:l]
        ap = a_row - jnp.log(dt_t)

        zpad = jnp.zeros((LANES - n_heads, l), F32)
        to_col = lambda v: jnp.concatenate([v, zpad], axis=0).T
        a_col = to_col(a_row)

        def pack(v):
            hi, mid, lo = _split_bf16(v, SPLIT_TERMS)
            return (hi + pltpu.roll(mid, n_heads, axis=1)
                    + pltpu.roll(lo, 2 * n_heads, axis=1)).astype(BF16)

        e3 = e3_ref[...]
        we_ref[0:l, :] = jnp.dot(pack(to_col(dt_t * jnp.exp(a_last - a_row))), e3,
                                 preferred_element_type=F32)
        we_ref[l:2 * l, :] = jnp.dot(pack(to_col(jnp.exp(a_row))), e3, preferred_element_type=F32)

        lane_g = lax.broadcasted_iota(jnp.int32, (l, gw), 1)
        for g in range(g_n):
            gs = slice(g * gw, (g + 1) * gw)
            b_g = xc_ref[:, width + g * n: width + (g + 1) * n].astype(BF16)
            c_g = xc_ref[:, width + (g_n + g) * n: width + (g_n + g + 1) * n].astype(BF16)
            cb = lax.dot_general(c_g, b_g, NT_DIMS, preferred_element_type=F32)
            cb = jnp.where(causal, cb, 0.0)
            xs_g = xc_ref[:, gs]
            xs_bf = xs_g.astype(BF16)

            scores = []
            xbd = []
            for k in range(hpg):
                h = g * hpg + k
                seg = a_col[:, h:h + 1] - ap[h:h + 1, :]
                scores.append((cb * jnp.exp(jnp.where(causal, seg, 0.0))).astype(BF16))
                in_head = (lane_g >= k * SSD_HEAD_DIM) & (lane_g < (k + 1) * SSD_HEAD_DIM)
                xbd.append(jnp.where(in_head, xs_bf, jnp.zeros_like(xs_bf)))
            s_cat = jnp.concatenate(scores, axis=1)
            x_bd = jnp.concatenate(xbd, axis=0)
            y_diag = jnp.dot(s_cat, x_bd, preferred_element_type=F32)

            h_prev = ht_ref[g]
            y_off = jnp.dot(c_g, h_prev.astype(BF16), preferred_element_type=F32) * we_ref[l:2 * l, gs]
            x_dec = (xs_g * we_ref[0:l, gs]).astype(BF16)
            st = lax.dot_general(b_g, x_dec, TN_DIMS, preferred_element_type=F32)
            ht_ref[g] = h_prev * we_ref[2 * l - 1:2 * l, gs] + st

            y = y_diag + y_off + dskip_ref[:, gs] * xs_g
            y = y * _silu(zb_ref[rows, gs]).astype(F32)
            ms = jnp.mean(y * y, axis=-1, keepdims=True)
            y_ref[rows, gs] = (y * lax.rsqrt(ms + EPS) * nw_ref[:, gs]).astype(y_ref.dtype)
        return carry

    def chunk_group(gi, carry):
        for k in range(SSD_UNROLL):
            carry = chunk(SSD_UNROLL * gi + k, carry)
        return carry

    lax.fori_loop(0, zb_ref.shape[0] // (SSD_UNROLL * l), chunk_group, 0)


def _ssd(proj, dt_t, dtb_col, alog_col, shift, conv_w, conv_b, dskip_e, norm_w, e3, w_branch2d,
         w_out, batch, seq, width):
    t = proj.shape[0]
    n_heads = dt_t.shape[0]
    l = SSD_CHUNK
    conv_k, conv_dim = conv_w.shape
    assert conv_dim == width + 2 * SSD_GROUPS * SSD_STATE
    assert width % SSD_GROUPS == 0 and conv_k - 1 <= l and SPLIT_TERMS * n_heads <= LANES
    assert shift.shape == ((conv_k - 1) * l, CONV_TAIL + l) and conv_k - 1 <= CONV_TAIL
    zx_w = width + conv_dim
    zx_blk = (3 * width) // zx_w
    assert zx_blk * zx_w == 3 * width
    tb = SSD_TB
    nb = seq // tb
    assert seq % tb == 0 and tb % l == 0
    n_steps = batch * nb
    wb_rows, wo_rows = w_branch2d.shape[0] // n_steps, w_out.shape[0] // n_steps
    assert wb_rows * n_steps == w_branch2d.shape[0] and wo_rows * n_steps == w_out.shape[0]
    assert wb_rows % (2 * SUBLANES) == 0 and wo_rows % (2 * SUBLANES) == 0
    tok = lambda b, c: b * nb + c
    const = lambda b, c: (0, 0)
    slab = lambda rows, arr: pl.BlockSpec((rows, arr.shape[1]), lambda b, c: (tok(b, c), 0))
    return pl.pallas_call(
        _ssd_kernel,
        grid=(batch, nb),
        in_specs=[
            pl.BlockSpec((tb, zx_w), lambda b, c: (tok(b, c), zx_blk)),
            pl.BlockSpec((n_heads, tb), lambda b, c: (0, tok(b, c))),
            pl.BlockSpec((n_heads, 1), const),
            pl.BlockSpec((n_heads, 1), const),
            pl.BlockSpec(shift.shape, const),
            pl.BlockSpec((conv_k, conv_dim), const),
            pl.BlockSpec((1, conv_dim), const),
            pl.BlockSpec((1, width), const),
            pl.BlockSpec((1, width), const),
            pl.BlockSpec((LANES, width), const),
            slab(wb_rows, w_branch2d),
            slab(wo_rows, w_out),
        ],
        out_specs=[
            pl.BlockSpec((tb, width), lambda b, c: (tok(b, c), 0)),
            slab(wb_rows, w_branch2d),
            slab(wo_rows, w_out),
        ],
        out_shape=[
            jax.ShapeDtypeStruct((t, width), BF16),
            jax.ShapeDtypeStruct(w_branch2d.shape, BF16),
            jax.ShapeDtypeStruct(w_out.shape, BF16),
        ],
        scratch_shapes=[
            pltpu.VMEM((CONV_TAIL + l, conv_dim), BF16),
            pltpu.VMEM((l, conv_dim), F32),
            pltpu.VMEM((SSD_GROUPS, SSD_STATE, width // SSD_GROUPS), F32),
            pltpu.VMEM((2 * l, width), F32),
        ],
        compiler_params=pltpu.CompilerParams(
            dimension_semantics=("arbitrary", "arbitrary"), vmem_limit_bytes=VMEM_LIMIT),
        name="ssd",
    )(proj, dt_t, dtb_col, alog_col, shift, conv_w, conv_b, dskip_e, norm_w, e3, w_branch2d, w_out)


def _merge_kernel(uvz_ref, lnw_ref, lnb_ref, ws_ref, bst_ref, yb_ref, gl_ref, x_ref,
                  bg_ref, wb_ref, wo_ref, fnw_ref, o_ref, ya_ref, *, final):
    d = x_ref.shape[1]
    e = yb_ref.shape[1]
    u_ref, v_ref, za_ref = (uvz_ref.at[:, k * e:(k + 1) * e] for k in range(3))
    bd1 = jnp.dot(yb_ref[...], wb_ref[e:2 * e, :], preferred_element_type=F32)
    _gmlp_rows(u_ref, v_ref, za_ref, lnw_ref, lnb_ref, ws_ref, bst_ref, ya_ref)
    bd0 = jnp.dot(ya_ref[...], wb_ref[0:e, :], preferred_element_type=F32)
    g0 = _sigmoid(gl_ref[:, 0:d].astype(F32) + bg_ref[0:1, :])
    g1 = _sigmoid(gl_ref[:, d:2 * d].astype(F32) + bg_ref[1:2, :])
    merged = (g0 * bd0 + g1 * bd1).astype(BF16)
    xn = x_ref[...] + jnp.dot(merged, wo_ref[...], preferred_element_type=F32)
    if final:
        ms = jnp.mean(xn * xn, axis=-1, keepdims=True)
        xn = xn * lax.rsqrt(ms + EPS) * fnw_ref[...]
    o_ref[...] = xn


def _merge(proj, ln_w, ln_b, w_s, b_s_t, y_b, x2d, b_gate, w_branch, w_out, final_norm_w, final):
    t, d = x2d.shape
    e = y_b.shape[1]
    tm = MERGE_TM
    g, l, _ = w_s.shape
    gate_blk = (proj.shape[1] - N_BRANCH * d) // (N_BRANCH * d)
    assert t % tm == 0 and gate_blk * N_BRANCH * d == proj.shape[1] - N_BRANCH * d
    assert l == GMLP_CHUNK and g == GMLP_GROUPS and tm % l == 0
    resident = dict(pipeline_mode=pl.Buffered(1))
    return pl.pallas_call(
        functools.partial(_merge_kernel, final=final),
        grid=(t // tm,),
        in_specs=[
            pl.BlockSpec((tm, 3 * e), lambda i: (i, 0)),
            pl.BlockSpec((1, e), lambda i: (0, 0)),
            pl.BlockSpec((1, e), lambda i: (0, 0)),
            pl.BlockSpec((g, l, l), lambda i: (0, 0, 0)),
            pl.BlockSpec((l, g), lambda i: (0, 0)),
            pl.BlockSpec((tm, e), lambda i: (i, 0)),
            pl.BlockSpec((tm, N_BRANCH * d), lambda i: (i, gate_blk)),
            pl.BlockSpec((tm, d), lambda i: (i, 0)),
            pl.BlockSpec((N_BRANCH, d), lambda i: (0, 0)),
            pl.BlockSpec((N_BRANCH * e, d), lambda i: (0, 0), **resident),
            pl.BlockSpec((d, d), lambda i: (0, 0), **resident),
            pl.BlockSpec((1, d), lambda i: (0, 0)),
        ],
        out_specs=pl.BlockSpec((tm, d), lambda i: (i, 0)),
        out_shape=jax.ShapeDtypeStruct((t, d), F32),
        scratch_shapes=[pltpu.VMEM((tm, e), BF16)],
        compiler_params=pltpu.CompilerParams(
            dimension_semantics=("arbitrary",), vmem_limit_bytes=VMEM_LIMIT),
        name="merge",
    )(proj, ln_w, ln_b, w_s, b_s_t, y_b, proj, x2d, b_gate, w_branch, w_out, final_norm_w)


def kernel(x, norm_w, w_in, b_gate, ln_v_w, ln_v_b, w_spatial, b_spatial, conv_w, conv_b,
           dt_bias, a_log, d_skip, ssm_norm_w, w_branch, w_out, final_norm_w):
    batch, seq, d = x.shape
    depth = norm_w.shape[0]
    width = ln_v_w.shape[1]
    n_heads = dt_bias.shape[1]
    conv_k, conv_dim = conv_w.shape[1:]
    assert width == d and n_heads * SSD_HEAD_DIM == width
    o_dt = 4 * width + conv_dim
    t = batch * seq
    x2d = x.reshape(t, d)

    head_of_col = jnp.arange(width, dtype=jnp.int32) // SSD_HEAD_DIM
    e_rows = jnp.arange(LANES, dtype=jnp.int32)
    e3 = ((e_rows[:, None] % n_heads == head_of_col[None, :])
          & (e_rows[:, None] < SPLIT_TERMS * n_heads)).astype(BF16)
    out_row = jnp.arange((conv_k - 1) * SSD_CHUNK, dtype=jnp.int32)
    slab, in_slab = out_row // ((conv_k - 1) * SHIFT_SLAB), out_row % ((conv_k - 1) * SHIFT_SLAB)
    tap, pos = in_slab // SHIFT_SLAB, slab * SHIFT_SLAB + in_slab % SHIFT_SLAB
    src = CONV_TAIL + pos - (conv_k - 1) + tap
    shift = (src[:, None]
             == jnp.arange(CONV_TAIL + SSD_CHUNK, dtype=jnp.int32)[None, :]).astype(BF16)

    for layer in range(depth):
        proj, dt_t = _inproj(x2d, norm_w[layer][None, :], jnp.swapaxes(w_in[layer], 0, 1), o_dt,
                             n_heads)

        y_b, w_branch_bf, w_out_bf = _ssd(
            proj, dt_t, dt_bias[layer][:, None], a_log[layer][:, None], shift, conv_w[layer],
            conv_b[layer][None, :], jnp.repeat(d_skip[layer], SSD_HEAD_DIM)[None, :],
            ssm_norm_w[layer][None, :], e3, w_branch[layer].reshape(N_BRANCH * width, d),
            w_out[layer], batch, seq, width)

        x2d = _merge(proj, ln_v_w[layer][None, :], ln_v_b[layer][None, :], w_spatial[layer],
                     b_spatial[layer].T, y_b, x2d, b_gate[layer], w_branch_bf, w_out_bf,
                     final_norm_w[None, :], final=(layer == depth - 1))
    return x2d.reshape(batch, seq, d)
```
